```python
import math
import jax
import jax.numpy as jnp
from jax import lax
import numpy as np

D_MODEL = 4096
BATCH = 4
SEQ = 4096
DEPTH = 4

HEAD_DIM = 128
N_MIXERS = 4
GQA_RATIO = 4
GRID_W = 64
Q_BLOCK = 128
ROPE_THETA = 10000.0
LN_EPS = 1e-5
RMS_EPS = 1e-6
DEEPNORM_ALPHA = (2 * DEPTH) ** 0.25
DEEPNORM_BETA = (8 * DEPTH) ** -0.25
N_MOD = 6

A_HEADS = D_MODEL // HEAD_DIM
A_KV = A_HEADS // GQA_RATIO
A_QKV = (A_HEADS + 2 * A_KV) * HEAD_DIM
B_PATTERNS = ((128, 1), (512, 4), (2048, 16))
B_HEADS = D_MODEL // (2 * HEAD_DIM)
B_KV = B_HEADS // GQA_RATIO
B_QKV = len(B_PATTERNS) * (B_HEADS + 2 * B_KV) * HEAD_DIM
C_HEADS = D_MODEL // HEAD_DIM
C_KV = C_HEADS // GQA_RATIO
C_WIN_ROWS = 8
C_WIN_COLS = 16
C_QKV = (C_HEADS + 2 * C_KV) * HEAD_DIM
D_HEADS = D_MODEL // (2 * HEAD_DIM)
D_KV = D_HEADS // GQA_RATIO
D_QKV = (2 * D_HEADS + 2 * D_KV + 2 * D_KV) * HEAD_DIM
N_EXPERTS = 64
TOP_K = 8
N_EXPERT_GROUPS = 8
TOPK_GROUPS = 4
EXPERT_HIDDEN = 3 * D_MODEL // 64
SHARED_HIDDEN = EXPERT_HIDDEN
ROUTED_SCALE = 2.5

kernel_name = 'hybrid_interleaved_encoder_moe'


def _rope_table(pos, dim):
    inv = ROPE_THETA ** (-jnp.arange(0, dim, 2, dtype=jnp.float32) / dim)
    ang = pos.astype(jnp.float32)[:, None] * inv[None, :]
    return jnp.cos(ang), jnp.sin(ang)


def _apply_rope(x, cos, sin):
    shp = (1, x.shape[1]) + (1,) * (x.ndim - 3) + (cos.shape[-1],)
    cos = cos.reshape(shp).astype(x.dtype)
    sin = sin.reshape(shp).astype(x.dtype)
    x1, x2 = jnp.split(x, 2, axis=-1)
    return jnp.concatenate([x1 * cos - x2 * sin, x1 * sin + x2 * cos], axis=-1)


def _axial_rope(x, row_cs, col_cs):
    half = x.shape[-1] // 2
    return jnp.concatenate([_apply_rope(x[..., :half], *row_cs), _apply_rope(x[..., half:], *col_cs)], axis=-1)


def _layer_norm(x, g, b):
    xf = x.astype(jnp.float32)
    xc = xf - xf.mean(-1, keepdims=True)
    var = (xc * xc).mean(-1, keepdims=True)
    return (xc * lax.rsqrt(var + LN_EPS) * g.astype(jnp.float32) + b.astype(jnp.float32)).astype(x.dtype)


def _rms_norm(x, g):
    xf = x.astype(jnp.float32)
    return (xf * lax.rsqrt((xf * xf).mean(-1, keepdims=True) + RMS_EPS) * g.astype(jnp.float32)).astype(x.dtype)


def _to_query_blocks(q):
    B, S = q.shape[:2]
    return q.reshape((B, S // Q_BLOCK, Q_BLOCK) + q.shape[2:]).swapaxes(0, 1)


def _from_query_blocks(o):
    nb, B, qb = o.shape[:3]
    return o.swapaxes(0, 1).reshape((B, nb * qb) + o.shape[3:])


def _gqa_axial_mixer(h, wqkv, wo, q_gain, k_gain, row_cs, col_cs):
    B, S, _ = h.shape
    G = A_HEADS // A_KV
    q, k, v = jnp.split(h @ wqkv, [A_HEADS * HEAD_DIM, (A_HEADS + A_KV) * HEAD_DIM], axis=-1)
    q = _axial_rope(_rms_norm(q.reshape(B, S, A_KV, G, HEAD_DIM), q_gain), row_cs, col_cs)
    k = _axial_rope(_rms_norm(k.reshape(B, S, A_KV, HEAD_DIM), k_gain), row_cs, col_cs)
    v = v.reshape(B, S, A_KV, HEAD_DIM)
    scale = HEAD_DIM ** -0.5

    def block(qb):
        s = jnp.einsum('bqkgd,bskd->bkgqs', qb, k).astype(jnp.float32) * scale
        p = jax.nn.softmax(s, axis=-1).astype(v.dtype)
        return jnp.einsum('bkgqs,bskd->bqkgd', p, v)

    o = _from_query_blocks(lax.map(block, _to_query_blocks(q)))
    return o.reshape(B, S, -1) @ wo


def _dilated_branch(q, k, v, dilation, half):
    B, S, H, Dh = q.shape
    KV = k.shape[2]
    G = H // KV
    L = S // dilation
    Bd = B * dilation

    def to_res(t):
        return t.reshape((B, L, dilation) + t.shape[2:]).swapaxes(1, 2).reshape((Bd, L) + t.shape[2:])

    qr, kr, vr = to_res(q), to_res(k), to_res(v)
    qb_len = min(Q_BLOCK, L)
    nb = -(-L // qb_len)
    Lp = nb * qb_len
    W = qb_len + 2 * half
    qr = jnp.pad(qr, ((0, 0), (0, Lp - L), (0, 0), (0, 0)))
    kv_pad = ((0, 0), (half, half + Lp - L), (0, 0), (0, 0))
    kr = jnp.pad(kr, kv_pad)
    vr = jnp.pad(vr, kv_pad)
    starts = jnp.arange(nb) * qb_len
    kidx = starts[:, None] + jnp.arange(W)[None, :]
    kpos = kidx - half
    qpos = starts[:, None] + jnp.arange(qb_len)[None, :]
    rel = kpos[:, None, :] - qpos[:, :, None]
    valid = (jnp.abs(rel) <= half) & (((kpos >= 0) & (kpos < L))[:, None, :] | (rel == 0))
    kw = kr[:, kidx]
    vw = vr[:, kidx]
    qb = qr.reshape(Bd, nb, qb_len, KV, G, Dh)
    s = jnp.einsum('bnqkgd,bnwkd->bnkgqw', qb, kw).astype(jnp.float32) * (Dh ** -0.5)
    s = jnp.where(valid[None, :, None, None], s, -jnp.inf)
    m = s.max(-1)
    p = jnp.exp(s - m[..., None])
    l = p.sum(-1)
    o = jnp.einsum('bnkgqw,bnwkd->bnqkgd', (p / l[..., None]).astype(v.dtype), vw)

    def from_res(t):
        t = t[:, :L]
        return t.reshape((B, dilation, L) + t.shape[2:]).swapaxes(1, 2).reshape((B, S) + t.shape[2:])

    o = from_res(o.reshape(Bd, Lp, H, Dh))
    m = from_res(jnp.moveaxis(m, -1, 2).reshape(Bd, Lp, H))
    l = from_res(jnp.moveaxis(l, -1, 2).reshape(Bd, Lp, H))
    return o, m, l


def _dilated_mixer(h, wqkv, wo, cs1):
    B, S, _ = h.shape
    NG = len(B_PATTERNS)
    q, k, v = jnp.split(h @ wqkv, [NG * B_HEADS * HEAD_DIM, NG * (B_HEADS + B_KV) * HEAD_DIM], axis=-1)
    q = _apply_rope(q.reshape(B, S, NG, B_HEADS, HEAD_DIM), *cs1)
    k = _apply_rope(k.reshape(B, S, NG, B_KV, HEAD_DIM), *cs1)
    v = v.reshape(B, S, NG, B_KV, HEAD_DIM)
    outs, maxes, dens = [], [], []
    for g, (window, dilation) in enumerate(B_PATTERNS):
        o, m, l = _dilated_branch(q[:, :, g], k[:, :, g], v[:, :, g], dilation, window // (2 * dilation))
        outs.append(o)
        maxes.append(m)
        dens.append(l)
    m_all = jnp.stack(maxes)
    w = jnp.stack(dens) * jnp.exp(m_all - m_all.max(0))
    w = w / w.sum(0)
    o = jnp.einsum('nbsh,nbshd->bshd', w.astype(h.dtype), jnp.stack(outs))
    return o.reshape(B, S, -1) @ wo


def _neighbourhood_mixer(h, wqkv, wo, rpb):
    B, S, _ = h.shape
    rows = S // GRID_W
    kr = min(C_WIN_ROWS, rows)
    G = C_HEADS // C_KV
    q, k, v = jnp.split(h @ wqkv, [C_HEADS * HEAD_DIM, (C_HEADS + C_KV) * HEAD_DIM], axis=-1)
    qg = q.reshape(B, rows, GRID_W, C_KV, G, HEAD_DIM)
    kg = k.reshape(B, rows, GRID_W, C_KV, HEAD_DIM)
    vg = v.reshape(B, rows, GRID_W, C_KV, HEAD_DIM)
    col = jnp.arange(GRID_W)
    col_idx = jnp.clip(col - C_WIN_COLS // 2, 0, GRID_W - C_WIN_COLS)[:, None] + jnp.arange(C_WIN_COLS)[None, :]
    dc = col_idx - col[:, None]
    col_bias = jnp.take(rpb, dc + C_WIN_COLS - 1, axis=2)
    scale = HEAD_DIM ** -0.5

    def row_block(r):
        rs = jnp.clip(r - kr // 2, 0, rows - kr)
        kw = lax.dynamic_slice_in_dim(kg, rs, kr, axis=1)[:, :, col_idx]
        vw = lax.dynamic_slice_in_dim(vg, rs, kr, axis=1)[:, :, col_idx]
        qr = lax.dynamic_index_in_dim(qg, r, axis=1, keepdims=False)
        dr = rs + jnp.arange(kr) - r
        bias = jnp.take(col_bias, dr + C_WIN_ROWS - 1, axis=1)
        bias = bias.transpose(0, 2, 1, 3).reshape(C_KV, G, GRID_W, kr, C_WIN_COLS)
        s = jnp.einsum('bckgd,brcjkd->bkgcrj', qr, kw).astype(jnp.float32) * scale + bias.astype(jnp.float32)
        p = jax.nn.softmax(s.reshape(s.shape[:4] + (-1,)), axis=-1).reshape(s.shape).astype(v.dtype)
        return jnp.einsum('bkgcrj,brcjkd->bckgd', p, vw)

    o = lax.map(row_block, jnp.arange(rows))
    return o.swapaxes(0, 1).reshape(B, S, -1) @ wo


def _diff_mixer(h, wqkv, wo, lam_params, sub_gain, cs1, layer_idx):
    B, S, _ = h.shape
    G = D_HEADS // D_KV
    q, k, v = jnp.split(h @ wqkv, [2 * D_HEADS * HEAD_DIM, 2 * (D_HEADS + D_KV) * HEAD_DIM], axis=-1)
    q = _apply_rope(q.reshape(B, S, D_KV, G, 2, HEAD_DIM), *cs1)
    k = _apply_rope(k.reshape(B, S, D_KV, 2, HEAD_DIM), *cs1)
    v = v.reshape(B, S, D_KV, 2 * HEAD_DIM)
    lam_init = 0.8 - 0.6 * math.exp(-0.3 * layer_idx)
    lp = lam_params.astype(jnp.float32)
    lam = jnp.exp(jnp.sum(lp[0] * lp[1])) - jnp.exp(jnp.sum(lp[2] * lp[3])) + lam_init
    scale = HEAD_DIM ** -0.5

    def block(qb):
        s = jnp.einsum('bqkgcd,bskcd->bkgcqs', qb, k).astype(jnp.float32) * scale
        p = jax.nn.softmax(s, axis=-1)
        pd = (p[:, :, :, 0] - lam * p[:, :, :, 1]).astype(v.dtype)
        return jnp.einsum('bkgqs,bskd->bqkgd', pd, v)

    o = _from_query_blocks(lax.map(block, _to_query_blocks(q)))
    o = _rms_norm(o, sub_gain) * (1.0 - lam_init)
    return o.reshape(B, S, -1) @ wo


def _moe(h, router_w, router_bias, w_gate, w_up, w_down, s_gate, s_up, s_down):
    B, S, D = h.shape
    t = h.reshape(-1, D)
    T = t.shape[0]
    scores = jax.nn.sigmoid((t @ router_w).astype(jnp.float32))
    biased = scores + router_bias.astype(jnp.float32)
    per_group = N_EXPERTS // N_EXPERT_GROUPS
    grp_score = lax.top_k(biased.reshape(T, N_EXPERT_GROUPS, per_group), 2)[0].sum(-1)
    _, top_groups = lax.top_k(grp_score, TOPK_GROUPS)
    group_mask = (top_groups[:, :, None] == jnp.arange(N_EXPERT_GROUPS)[None, None, :]).any(1)
    expert_mask = jnp.repeat(group_mask, per_group, axis=1)
    _, top_e = lax.top_k(jnp.where(expert_mask, biased, -jnp.inf), TOP_K)
    w = jnp.take_along_axis(scores, top_e, axis=-1)
    w = w / w.sum(-1, keepdims=True) * ROUTED_SCALE
    gates = jnp.zeros((T, N_EXPERTS), jnp.float32).at[jnp.arange(T)[:, None], top_e].set(w)
    shared = (jax.nn.silu(t @ s_gate) * (t @ s_up)) @ s_down

    def add_expert(acc, e):
        wg, wu, wd, g = e
        y = (jax.nn.silu(t @ wg) * (t @ wu)) @ wd
        return acc + g[:, None].astype(y.dtype) * y, None

    out, _ = lax.scan(add_expert, shared, (w_gate, w_up, w_down, gates.T))
    return out.reshape(B, S, D)


def setup_inputs(seed: int = 0) -> dict:
    key = jax.random.key(seed)
    ks = iter(jax.random.split(key, 32))

    def nrm(shape, scale):
        return jax.random.normal(next(ks), shape, jnp.float32) * scale

    n_of = [len(range(m, DEPTH, N_MIXERS)) for m in range(N_MIXERS)]
    D = D_MODEL
    gate_offset = jnp.array([0.0, 0.0, 1.0, 0.0, 0.0, 1.0], jnp.float32)[None, :, None]
    return {
        'x': nrm((BATCH, SEQ, D), 1.0),
        'c': nrm((BATCH, D), 1.0),
        'w_ada': nrm((D, N_MOD * D), 0.1 * D ** -0.5),
        'b_ada': nrm((N_MOD * D,), 0.01),
        'ada_table': nrm((DEPTH, N_MOD, D), 0.1) + gate_offset,
        'ln_gain': 1.0 + nrm((DEPTH, 2, D), 0.02),
        'ln_bias': nrm((DEPTH, 2, D), 0.02),
        'gqa_wqkv': nrm((n_of[0], D, A_QKV), D ** -0.5),
        'gqa_wo': nrm((n_of[0], A_HEADS * HEAD_DIM, D), (A_HEADS * HEAD_DIM) ** -0.5 * DEEPNORM_BETA),
        'gqa_q_gain': 1.0 + nrm((n_of[0], HEAD_DIM), 0.02),
        'gqa_k_gain': 1.0 + nrm((n_of[0], HEAD_DIM), 0.02),
        'dil_wqkv': nrm((n_of[1], D, B_QKV), D ** -0.5),
        'dil_wo': nrm((n_of[1], B_HEADS * HEAD_DIM, D), (B_HEADS * HEAD_DIM) ** -0.5 * DEEPNORM_BETA),
        'nat_wqkv': nrm((n_of[2], D, C_QKV), D ** -0.5),
        'nat_wo': nrm((n_of[2], C_HEADS * HEAD_DIM, D), (C_HEADS * HEAD_DIM) ** -0.5 * DEEPNORM_BETA),
        'nat_rpb': nrm((n_of[2], C_HEADS, 2 * C_WIN_ROWS - 1, 2 * C_WIN_COLS - 1), 0.1),
        'dif_wqkv': nrm((n_of[3], D, D_QKV), D ** -0.5),
        'dif_wo': nrm((n_of[3], D_HEADS * 2 * HEAD_DIM, D), (D_HEADS * 2 * HEAD_DIM) ** -0.5 * DEEPNORM_BETA),
        'dif_lambda': nrm((n_of[3], 4, HEAD_DIM), 0.1),
        'dif_subln_gain': 1.0 + nrm((n_of[3], 2 * HEAD_DIM), 0.02),
        'router_w': nrm((DEPTH, D, N_EXPERTS), D ** -0.5),
        'router_bias': nrm((DEPTH, N_EXPERTS), 0.01),
        'exp_w_gate': nrm((DEPTH, N_EXPERTS, D, EXPERT_HIDDEN), D ** -0.5),
        'exp_w_up': nrm((DEPTH, N_EXPERTS, D, EXPERT_HIDDEN), D ** -0.5),
        'exp_w_down': nrm((DEPTH, N_EXPERTS, EXPERT_HIDDEN, D), EXPERT_HIDDEN ** -0.5 * DEEPNORM_BETA),
        'sh_w_gate': nrm((DEPTH, D, SHARED_HIDDEN), D ** -0.5),
        'sh_w_up': nrm((DEPTH, D, SHARED_HIDDEN), D ** -0.5),
        'sh_w_down': nrm((DEPTH, SHARED_HIDDEN, D), SHARED_HIDDEN ** -0.5 * DEEPNORM_BETA),
    }


def reference(x, c, w_ada, b_ada, ada_table, ln_gain, ln_bias, gqa_wqkv, gqa_wo, gqa_q_gain, gqa_k_gain,
              dil_wqkv, dil_wo, nat_wqkv, nat_wo, nat_rpb, dif_wqkv, dif_wo, dif_lambda, dif_subln_gain,
              router_w, router_bias, exp_w_gate, exp_w_up, exp_w_down, sh_w_gate, sh_w_up, sh_w_down):
    B, S, D = x.shape
    pos = jnp.arange(S)
    cs1 = _rope_table(pos, HEAD_DIM)
    row_cs = _rope_table(pos // GRID_W, HEAD_DIM // 2)
    col_cs = _rope_table(pos % GRID_W, HEAD_DIM // 2)
    mod_shared = (jax.nn.silu(c) @ w_ada + b_ada).reshape(B, N_MOD, D)
    for i in range(DEPTH):
        kind, j = i % N_MIXERS, i // N_MIXERS
        mod = (mod_shared + ada_table[i])[:, :, None, :]
        h = x * (1 + mod[:, 1]) + mod[:, 0]
        if kind == 0:
            y = _gqa_axial_mixer(h, gqa_wqkv[j], gqa_wo[j], gqa_q_gain[j], gqa_k_gain[j], row_cs, col_cs)
        elif kind == 1:
            y = _dilated_mixer(h, dil_wqkv[j], dil_wo[j], cs1)
        elif kind == 2:
            y = _neighbourhood_mixer(h, nat_wqkv[j], nat_wo[j], nat_rpb[j])
        else:
            y = _diff_mixer(h, dif_wqkv[j], dif_wo[j], dif_lambda[j], dif_subln_gain[j], cs1, i)
        x = _layer_norm(DEEPNORM_ALPHA * x + mod[:, 2] * y, ln_gain[i, 0], ln_bias[i, 0])
        h = x * (1 + mod[:, 4]) + mod[:, 3]
        y = _moe(h, router_w[i], router_bias[i], exp_w_gate[i], exp_w_up[i], exp_w_down[i],
                 sh_w_gate[i], sh_w_up[i], sh_w_down[i])
        x = _layer_norm(DEEPNORM_ALPHA * x + mod[:, 5] * y, ln_gain[i, 1], ln_bias[i, 1])
    return x
```

```python
import functools
import math

import jax
import jax.numpy as jnp
from jax import lax
from jax.experimental import pallas as pl
from jax.experimental.pallas import tpu as pltpu

HEAD_DIM = 128
N_MIXERS = 4
GQA_RATIO = 4
GRID_W = 64
ROPE_THETA = 10000.0
LN_EPS = 1e-5
RMS_EPS = 1e-6
N_MOD = 6
DIL_PATTERNS = ((128, 1), (512, 4), (2048, 16))
NAT_WIN_ROWS = 8
NAT_WIN_COLS = 16
N_EXPERT_GROUPS = 8
TOPK_GROUPS = 4
TOP_K = 8
ROUTED_SCALE = 2.5

VMEM_LIMIT_BYTES = 56 * 1024 * 1024
NEG_BIG = -1e30
GATHER_CHUNK = 1024

F32 = jnp.float32
BF16 = jnp.bfloat16


def _cparams(*sem):
    return pltpu.CompilerParams(dimension_semantics=("arbitrary",) * len(sem), vmem_limit_bytes=VMEM_LIMIT_BYTES)


def _tile(n, pref):
    if n <= pref:
        return n
    while n % pref:
        pref //= 2
    return pref


def _dot(a, b):
    return jnp.dot(a, b, preferred_element_type=F32)


def _dot_nt(a, b):
    return lax.dot_general(a, b, (((1,), (1,)), ((), ())), preferred_element_type=F32)


def _matmul_kernel(a_ref, w_ref, o_ref, acc_ref, *, nk):
    k = pl.program_id(2)

    @pl.when(k == 0)
    def _():
        acc_ref[...] = jnp.zeros_like(acc_ref)

    acc_ref[...] += _dot(a_ref[...].astype(BF16), w_ref[...].astype(BF16))

    @pl.when(k == nk - 1)
    def _():
        o_ref[...] = acc_ref[...].astype(o_ref.dtype)


def _matmul(a, w, out_dtype, tm=2048, tn=1024, tk=512, name="matmul"):
    m, kd = a.shape
    n = w.shape[1]
    tm, tn, tk = _tile(m, tm), _tile(n, tn), _tile(kd, tk)
    return pl.pallas_call(
        functools.partial(_matmul_kernel, nk=kd // tk),
        grid=(m // tm, n // tn, kd // tk),
        in_specs=[pl.BlockSpec((tm, tk), lambda i, j, k: (i, k)),
                  pl.BlockSpec((tk, tn), lambda i, j, k: (k, j))],
        out_specs=pl.BlockSpec((tm, tn), lambda i, j, k: (i, j)),
        out_shape=jax.ShapeDtypeStruct((m, n), out_dtype),
        scratch_shapes=[pltpu.VMEM((tm, tn), F32)],
        compiler_params=_cparams("parallel", "parallel", "arbitrary"),
        name=name,
    )(a, w)


def _ada_kernel(c_ref, w_ref, b_ref, o_ref, acc_ref, *, nk):
    k = pl.program_id(1)

    @pl.when(k == 0)
    def _():
        acc_ref[...] = jnp.zeros_like(acc_ref)

    c = c_ref[...]
    a = c * jax.nn.sigmoid(c)
    acc_ref[...] += _dot(a.astype(BF16), w_ref[...].astype(BF16))

    @pl.when(k == nk - 1)
    def _():
        o_ref[...] = acc_ref[...] + b_ref[...]


def _ada_proj(c_pad, w_ada, b_ada, tn=2048, tk=512):
    m, kd = c_pad.shape
    n = w_ada.shape[1]
    tn, tk = min(tn, n), min(tk, kd)
    return pl.pallas_call(
        functools.partial(_ada_kernel, nk=kd // tk),
        grid=(n // tn, kd // tk),
        in_specs=[pl.BlockSpec((m, tk), lambda j, k: (0, k)),
                  pl.BlockSpec((tk, tn), lambda j, k: (k, j)),
                  pl.BlockSpec((1, tn), lambda j, k: (0, j))],
        out_specs=pl.BlockSpec((m, tn), lambda j, k: (0, j)),
        out_shape=jax.ShapeDtypeStruct((m, n), F32),
        scratch_shapes=[pltpu.VMEM((m, tn), F32)],
        compiler_params=_cparams("parallel", "arbitrary"),
        name="ada_proj",
    )(c_pad, w_ada, b_ada.reshape(1, n))


def _modulate_kernel(x_ref, shift_ref, scale_ref, h_ref):
    h_ref[...] = (x_ref[...] * (1.0 + scale_ref[...]) + shift_ref[...]).astype(h_ref.dtype)


def _modulate(x, shift, scale, ts=512):
    b, s, d = x.shape
    ts = min(ts, s)
    row = pl.BlockSpec((None, ts, d), lambda bi, i: (bi, i, 0))
    vec = pl.BlockSpec((None, 1, d), lambda bi, i: (bi, 0, 0))
    return pl.pallas_call(
        _modulate_kernel,
        grid=(b, s // ts),
        in_specs=[row, vec, vec],
        out_specs=row,
        out_shape=jax.ShapeDtypeStruct((b, s, d), BF16),
        compiler_params=_cparams("parallel", "parallel"),
        name="modulate",
    )(x, shift, scale)


def _ln_kernel(*refs, alpha, n_routed, with_h):
    it = iter(refs)
    x_ref, y_ref = next(it), next(it)
    if n_routed:
        yk_ref, wk_ref = next(it), next(it)
    gate_ref, g_ref, b_ref = next(it), next(it), next(it)
    if with_h:
        shift_ref, scale_ref = next(it), next(it)
    xo_ref = next(it)
    if with_h:
        h_ref = next(it)

    y = y_ref[...].astype(F32)
    if n_routed:
        wk = wk_ref[...]
        for k in range(n_routed):
            y = y + wk[:, k:k + 1] * yk_ref[k].astype(F32)
    z = alpha * x_ref[...] + gate_ref[...] * y
    zc = z - jnp.mean(z, axis=-1, keepdims=True)
    var = jnp.mean(zc * zc, axis=-1, keepdims=True)
    xn = zc * lax.rsqrt(var + LN_EPS) * g_ref[...] + b_ref[...]
    xo_ref[...] = xn
    if with_h:
        h_ref[...] = (xn * (1.0 + scale_ref[...]) + shift_ref[...]).astype(h_ref.dtype)


def _ln_residual(x, y, gate, g, b, alpha, shift=None, scale=None, yk=None, wk=None, ts=128):
    bsz, s, d = x.shape
    ts = min(ts, s)
    with_h = shift is not None
    n_routed = 0 if yk is None else yk.shape[0]
    row = pl.BlockSpec((None, ts, d), lambda bi, i: (bi, i, 0))
    vec = pl.BlockSpec((None, 1, d), lambda bi, i: (bi, 0, 0))
    par = pl.BlockSpec((1, d), lambda bi, i: (0, 0))
    args, specs = [x, y], [row, row]
    if n_routed:
        args += [yk, wk]
        specs += [pl.BlockSpec((n_routed, None, ts, d), lambda bi, i: (0, bi, i, 0)),
                  pl.BlockSpec((None, ts, n_routed), lambda bi, i: (bi, i, 0))]
    args += [gate, g.reshape(1, d), b.reshape(1, d)]
    specs += [vec, par, par]
    if with_h:
        args += [shift, scale]
        specs += [vec, vec]
    out_shape = [jax.ShapeDtypeStruct((bsz, s, d), F32)]
    out_specs = [row]
    if with_h:
        out_shape.append(jax.ShapeDtypeStruct((bsz, s, d), BF16))
        out_specs.append(row)
    outs = pl.pallas_call(
        functools.partial(_ln_kernel, alpha=alpha, n_routed=n_routed, with_h=with_h),
        grid=(bsz, s // ts),
        in_specs=specs,
        out_specs=out_specs,
        out_shape=out_shape,
        compiler_params=_cparams("parallel", "parallel"),
        name="ln_moe" if n_routed else "ln_mixer",
    )(*args)
    return (outs[0], outs[1]) if with_h else (outs[0], None)


def _prep_kernel(flags_ref, x_ref, gain_ref, cos_ref, sin_ref, o_ref, *, half):
    j = pl.program_id(1)
    flag = flags_ref[j]
    x = x_ref[...].astype(F32)
    xn = x * lax.rsqrt(jnp.mean(x * x, axis=-1, keepdims=True) + RMS_EPS) * gain_ref[...]
    x = jnp.where((flag & 1) == 1, xn, x)
    lane = lax.broadcasted_iota(jnp.int32, x.shape, 1)
    if 2 * half == HEAD_DIM:
        partner = pltpu.roll(x, half, 1)
    else:
        up = pltpu.roll(x, HEAD_DIM - half, 1)
        dn = pltpu.roll(x, half, 1)
        partner = jnp.where((lane % (2 * half)) < half, up, dn)
    xr = x * cos_ref[...] + partner * sin_ref[...]
    x = jnp.where((flag & 2) == 2, xr, x)
    o_ref[...] = x.astype(o_ref.dtype)


def _qk_prep(qkv, flags, gains, cos_t, sin_t, half, seq, ts=512):
    t, n = qkv.shape
    nc = n // HEAD_DIM
    ts = min(ts, seq)
    nsb = seq // ts
    return pl.pallas_call(
        functools.partial(_prep_kernel, half=half),
        grid_spec=pltpu.PrefetchScalarGridSpec(
            num_scalar_prefetch=1,
            grid=(t // ts, nc),
            in_specs=[pl.BlockSpec((ts, HEAD_DIM), lambda i, j, f: (i, j)),
                      pl.BlockSpec((None, 1, HEAD_DIM), lambda i, j, f: (j, 0, 0)),
                      pl.BlockSpec((ts, HEAD_DIM), lambda i, j, f: (i % nsb, 0)),
                      pl.BlockSpec((ts, HEAD_DIM), lambda i, j, f: (i % nsb, 0))],
            out_specs=pl.BlockSpec((ts, HEAD_DIM), lambda i, j, f: (i, j)),
        ),
        out_shape=jax.ShapeDtypeStruct((t, n), BF16),
        compiler_params=_cparams("parallel", "parallel"),
        name="qk_prep",
    )(flags, qkv, gains.reshape(nc, 1, HEAD_DIM), cos_t, sin_t)


def _rope_tables(pos, dim):
    inv = ROPE_THETA ** (-jnp.arange(0, dim, 2, dtype=F32) / dim)
    ang = pos.astype(F32)[:, None] * inv[None, :]
    return jnp.cos(ang), jnp.sin(ang)


def _softmax_pv(s, v):
    m = jnp.max(s, axis=-1, keepdims=True)
    p = jnp.exp(s - m)
    l = jnp.sum(p, axis=-1, keepdims=True)
    return _dot(p.astype(BF16), v) / l


def _gqa_attn_kernel(q_ref, k_ref, v_ref, o_ref, *, n_q, scale):
    k = k_ref[...]
    v = v_ref[...]
    for j in range(n_q):
        sl = slice(j * HEAD_DIM, (j + 1) * HEAD_DIM)
        s = _dot_nt(q_ref[:, sl], k) * scale
        o_ref[:, sl] = _softmax_pv(s, v).astype(o_ref.dtype)


def _diff_attn_kernel(q_ref, k_ref, v_ref, lam_ref, gain_ref, o_ref, *, n_g, scale, lam_init):
    lp = lam_ref[...]
    lam = (jnp.exp(jnp.sum(lp[0:1] * lp[1:2], axis=-1, keepdims=True))
           - jnp.exp(jnp.sum(lp[2:3] * lp[3:4], axis=-1, keepdims=True)) + lam_init)
    v = v_ref[...]
    dv = v.shape[-1]
    for g in range(n_g):
        o = None
        for c in range(2):
            j = 2 * g + c
            q = q_ref[:, j * HEAD_DIM:(j + 1) * HEAD_DIM]
            s = _dot_nt(q, k_ref[:, c * HEAD_DIM:(c + 1) * HEAD_DIM]) * scale
            oc = _softmax_pv(s, v)
            o = oc if c == 0 else o - lam * oc
        o = o * lax.rsqrt(jnp.mean(o * o, axis=-1, keepdims=True) + RMS_EPS) * gain_ref[...]
        o_ref[:, g * dv:(g + 1) * dv] = (o * (1.0 - lam_init)).astype(o_ref.dtype)


def _gqa_attention(qkv, bsz, seq, n_kv, tq=256):
    g = GQA_RATIO
    n_h = n_kv * g
    tq = min(tq, seq)
    nq = seq // tq
    qw = g * HEAD_DIM
    return pl.pallas_call(
        functools.partial(_gqa_attn_kernel, n_q=g, scale=HEAD_DIM ** -0.5),
        grid=(bsz, n_kv, nq),
        in_specs=[pl.BlockSpec((tq, qw), lambda b, h, i: (b * nq + i, h)),
                  pl.BlockSpec((seq, HEAD_DIM), lambda b, h, i: (b, n_h + h)),
                  pl.BlockSpec((seq, HEAD_DIM), lambda b, h, i: (b, n_h + n_kv + h))],
        out_specs=pl.BlockSpec((tq, qw), lambda b, h, i: (b * nq + i, h)),
        out_shape=jax.ShapeDtypeStruct((bsz * seq, n_h * HEAD_DIM), BF16),
        compiler_params=_cparams("parallel", "parallel", "arbitrary"),
        name="gqa_attention",
    )(qkv, qkv, qkv)


def _diff_attention(qkv, lam_params, sub_gain, lam_init, bsz, seq, n_kv, tq=256):
    g = GQA_RATIO
    n_h = n_kv * g
    tq = min(tq, seq)
    nq = seq // tq
    qw = g * 2 * HEAD_DIM
    dv = 2 * HEAD_DIM
    k_blk0 = (2 * n_h * HEAD_DIM) // dv
    v_blk0 = k_blk0 + n_kv
    return pl.pallas_call(
        functools.partial(_diff_attn_kernel, n_g=g, scale=HEAD_DIM ** -0.5, lam_init=lam_init),
        grid=(bsz, n_kv, nq),
        in_specs=[pl.BlockSpec((tq, qw), lambda b, h, i: (b * nq + i, h)),
                  pl.BlockSpec((seq, dv), lambda b, h, i: (b, k_blk0 + h)),
                  pl.BlockSpec((seq, dv), lambda b, h, i: (b, v_blk0 + h)),
                  pl.BlockSpec((4, HEAD_DIM), lambda b, h, i: (0, 0)),
                  pl.BlockSpec((1, dv), lambda b, h, i: (0, 0))],
        out_specs=pl.BlockSpec((tq, g * dv), lambda b, h, i: (b * nq + i, h)),
        out_shape=jax.ShapeDtypeStruct((bsz * seq, n_h * dv), BF16),
        compiler_params=_cparams("parallel", "parallel", "arbitrary"),
        name="diff_attention",
    )(qkv, qkv, qkv, lam_params.astype(F32), sub_gain.reshape(1, dv).astype(F32))


def _dil_attn_kernel(q_ref, k_ref, v_ref, o_ref, st_ref, *, n_g, half, sub, scale):
    tl = q_ref.shape[0]
    length = k_ref.shape[0]
    win = min(length, sub + 2 * half)
    i = pl.program_id(3)
    lane = lax.broadcasted_iota(jnp.int32, (sub, HEAD_DIM), 1)
    for sb in range(tl // sub):
        qs = i * tl + sb * sub
        start = jnp.clip(qs - half, 0, length - win)
        start = pl.multiple_of(start, half)
        kw = k_ref[pl.ds(start, win), :]
        vw = v_ref[pl.ds(start, win), :]
        kpos = start + lax.broadcasted_iota(jnp.int32, (sub, win), 1)
        qpos = qs + lax.broadcasted_iota(jnp.int32, (sub, win), 0)
        valid = jnp.abs(kpos - qpos) <= half
        stats = jnp.zeros((sub, HEAD_DIM), F32)
        for g in range(n_g):
            sl = slice(g * HEAD_DIM, (g + 1) * HEAD_DIM)
            s = _dot_nt(q_ref[sb * sub:(sb + 1) * sub, sl], kw) * scale
            s = jnp.where(valid, s, NEG_BIG)
            m = jnp.max(s, axis=-1, keepdims=True)
            p = jnp.exp(s - m)
            l = jnp.sum(p, axis=-1, keepdims=True)
            o_ref[sb * sub:(sb + 1) * sub, sl] = (_dot(p.astype(BF16), vw) / l).astype(o_ref.dtype)
            stats = jnp.where(lane == g, m, stats)
            stats = jnp.where(lane == n_g + g, l, stats)
        st_ref[sb * sub:(sb + 1) * sub, :] = stats


def _dilated_group(qkv, grp, n_grp, window, dilation, bsz, seq, n_kv, tl=512, sub=128):
    g = GQA_RATIO
    n_h = n_kv * g
    d = dilation
    length = seq // d
    half = window // (2 * d)
    sub = min(sub, length)
    tl = min(tl, length)
    nl = length // tl
    n_cols = qkv.shape[1]
    qkv_v = qkv.reshape(bsz * length, d * n_cols)
    qw = g * HEAD_DIM
    assert n_cols % qw == 0 and length % tl == 0 and tl % sub == 0
    q_blk = lambda b, r, h, i: (b * nl + i, (r * n_cols + grp * n_h * HEAD_DIM) // qw + h)
    k_col0 = n_grp * n_h * HEAD_DIM + grp * n_kv * HEAD_DIM
    v_col0 = n_grp * (n_h + n_kv) * HEAD_DIM + grp * n_kv * HEAD_DIM
    k_blk = lambda b, r, h, i: (b, (r * n_cols + k_col0) // HEAD_DIM + h)
    v_blk = lambda b, r, h, i: (b, (r * n_cols + v_col0) // HEAD_DIM + h)
    o, st = pl.pallas_call(
        functools.partial(_dil_attn_kernel, n_g=g, half=half, sub=sub, scale=HEAD_DIM ** -0.5),
        grid=(bsz, d, n_kv, nl),
        in_specs=[pl.BlockSpec((tl, qw), q_blk),
                  pl.BlockSpec((length, HEAD_DIM), k_blk),
                  pl.BlockSpec((length, HEAD_DIM), v_blk)],
        out_specs=[pl.BlockSpec((tl, qw), lambda b, r, h, i: (b * nl + i, r * n_kv + h)),
                   pl.BlockSpec((tl, HEAD_DIM), lambda b, r, h, i: (b * nl + i, r * n_kv + h))],
        out_shape=[jax.ShapeDtypeStruct((bsz * length, d * n_h * HEAD_DIM), F32),
                   jax.ShapeDtypeStruct((bsz * length, d * n_kv * HEAD_DIM), F32)],
        compiler_params=_cparams("parallel", "parallel", "parallel", "arbitrary"),
        name=f"dilated_attention_d{d}",
    )(qkv_v, qkv_v, qkv_v)
    return o.reshape(bsz * seq, n_h * HEAD_DIM), st.reshape(bsz * seq, n_kv * HEAD_DIM)


def _dil_merge_kernel(*refs, n_grp, n_kv, n_g):
    o_refs, st_refs, out_ref = refs[:n_grp], refs[n_grp:2 * n_grp], refs[2 * n_grp]
    for kv in range(n_kv):
        st = [r[:, kv * HEAD_DIM:(kv + 1) * HEAD_DIM] for r in st_refs]
        for g in range(n_g):
            h = kv * n_g + g
            ms = [s[:, g:g + 1] for s in st]
            ls = [s[:, n_g + g:n_g + g + 1] for s in st]
            mmax = functools.reduce(jnp.maximum, ms)
            ws = [l * jnp.exp(m - mmax) for m, l in zip(ms, ls)]
            wsum = functools.reduce(lambda a, b: a + b, ws)
            sl = slice(h * HEAD_DIM, (h + 1) * HEAD_DIM)
            acc = None
            for w, o_ref in zip(ws, o_refs):
                term = (w / wsum) * o_ref[:, sl]
                acc = term if acc is None else acc + term
            out_ref[:, sl] = acc.astype(out_ref.dtype)


def _dilated_merge(outs, stats, n_kv, ts=256):
    t, n = outs[0].shape
    ts = min(ts, t)
    n_grp = len(outs)
    o_spec = pl.BlockSpec((ts, n), lambda i: (i, 0))
    s_spec = pl.BlockSpec((ts, stats[0].shape[1]), lambda i: (i, 0))
    return pl.pallas_call(
        functools.partial(_dil_merge_kernel, n_grp=n_grp, n_kv=n_kv, n_g=GQA_RATIO),
        grid=(t // ts,),
        in_specs=[o_spec] * n_grp + [s_spec] * n_grp,
        out_specs=o_spec,
        out_shape=jax.ShapeDtypeStruct((t, n), BF16),
        compiler_params=_cparams("parallel"),
        name="dilated_merge",
    )(*outs, *stats)


def _nat_attn_kernel(q_ref, k_ref, v_ref, bias_ref, o_ref, *, n_g, rows, scale):
    rows_per_step = q_ref.shape[0] // GRID_W
    i = pl.program_id(2)
    nk = NAT_WIN_ROWS * GRID_W
    for rr in range(rows_per_step):
        r = i * rows_per_step + rr
        rs = jnp.clip(r - NAT_WIN_ROWS // 2, 0, rows - NAT_WIN_ROWS)
        ro = r - rs
        start = pl.multiple_of(rs * GRID_W, GRID_W)
        kw = k_ref[pl.ds(start, nk), :]
        vw = v_ref[pl.ds(start, nk), :]
        rsl = slice(rr * GRID_W, (rr + 1) * GRID_W)
        q = jnp.concatenate([q_ref[rsl, g * HEAD_DIM:(g + 1) * HEAD_DIM] for g in range(n_g)], axis=0)
        s = _dot_nt(q, kw) * scale + bias_ref[ro]
        o = _softmax_pv(s, vw)
        for g in range(n_g):
            o_ref[rsl, g * HEAD_DIM:(g + 1) * HEAD_DIM] = o[g * GRID_W:(g + 1) * GRID_W].astype(o_ref.dtype)


def _nat_bias_table(rpb, n_kv):
    n_h = rpb.shape[0]
    g = n_h // n_kv
    ro = jnp.arange(NAT_WIN_ROWS)
    ki = jnp.arange(NAT_WIN_ROWS)
    row_idx = ki[None, :] - ro[:, None] + NAT_WIN_ROWS - 1
    c = jnp.arange(GRID_W)
    kc = jnp.arange(GRID_W)
    cs = jnp.clip(c - NAT_WIN_COLS // 2, 0, GRID_W - NAT_WIN_COLS)
    valid = (kc[None, :] >= cs[:, None]) & (kc[None, :] < cs[:, None] + NAT_WIN_COLS)
    col_idx = jnp.clip(kc[None, :] - c[:, None] + NAT_WIN_COLS - 1, 0, 2 * NAT_WIN_COLS - 2)
    tab = rpb.astype(F32)[:, row_idx[:, :, None, None], col_idx[None, None, :, :]]
    tab = jnp.where(valid[None, None, None], tab, NEG_BIG)
    tab = tab.transpose(0, 1, 3, 2, 4).reshape(n_kv, g, NAT_WIN_ROWS, GRID_W, NAT_WIN_ROWS * GRID_W)
    return tab.transpose(0, 2, 1, 3, 4).reshape(n_kv, NAT_WIN_ROWS, g * GRID_W, NAT_WIN_ROWS * GRID_W)


def _nat_attention(qkv, bias_tab, bsz, seq, n_kv, rows_per_step=8):
    g = GQA_RATIO
    n_h = n_kv * g
    rows = seq // GRID_W
    assert rows >= NAT_WIN_ROWS
    rows_per_step = min(rows_per_step, rows)
    tq = rows_per_step * GRID_W
    nq = seq // tq
    qw = g * HEAD_DIM
    return pl.pallas_call(
        functools.partial(_nat_attn_kernel, n_g=g, rows=rows, scale=HEAD_DIM ** -0.5),
        grid=(bsz, n_kv, nq),
        in_specs=[pl.BlockSpec((tq, qw), lambda b, h, i: (b * nq + i, h)),
                  pl.BlockSpec((seq, HEAD_DIM), lambda b, h, i: (b, n_h + h)),
                  pl.BlockSpec((seq, HEAD_DIM), lambda b, h, i: (b, n_h + n_kv + h)),
                  pl.BlockSpec((None,) + bias_tab.shape[1:], lambda b, h, i: (h, 0, 0, 0))],
        out_specs=pl.BlockSpec((tq, qw), lambda b, h, i: (b * nq + i, h)),
        out_shape=jax.ShapeDtypeStruct((bsz * seq, n_h * HEAD_DIM), BF16),
        compiler_params=_cparams("parallel", "parallel", "arbitrary"),
        name="nat_attention",
    )(qkv, qkv, qkv, bias_tab)


def _router_kernel(h_ref, w_ref, bias_ref, e_ref, wt_ref, rank_ref, cnt_ref, carry_ref, *, n_e):
    i = pl.program_id(0)
    tm = h_ref.shape[0]
    per_group = n_e // N_EXPERT_GROUPS

    @pl.when(i == 0)
    def _():
        carry_ref[...] = jnp.zeros_like(carry_ref)

    h = h_ref[...]
    w = w_ref[...]
    w_hi = w.astype(BF16)
    w_lo = (w - w_hi.astype(F32)).astype(BF16)
    logits = _dot_nt(w_hi, h) + _dot_nt(w_lo, h)
    scores = jax.nn.sigmoid(logits)
    biased = scores + bias_ref[...]

    b3 = biased.reshape(N_EXPERT_GROUPS, per_group, tm)
    idx3 = lax.broadcasted_iota(jnp.int32, b3.shape, 1)
    m1 = jnp.max(b3, axis=1, keepdims=True)
    i1 = jnp.min(jnp.where(b3 == m1, idx3, per_group), axis=1, keepdims=True)
    m2 = jnp.max(jnp.where(idx3 == i1, -jnp.inf, b3), axis=1, keepdims=True)
    gs = (m1 + m2).reshape(N_EXPERT_GROUPS, tm)

    gidx = lax.broadcasted_iota(jnp.int32, gs.shape, 0)
    gsel = jnp.zeros(gs.shape, jnp.bool_)
    for _ in range(TOPK_GROUPS):
        gm = jnp.max(gs, axis=0, keepdims=True)
        gi = jnp.min(jnp.where(gs == gm, gidx, N_EXPERT_GROUPS), axis=0, keepdims=True)
        hit = gidx == gi
        gsel = gsel | hit
        gs = jnp.where(hit, -jnp.inf, gs)
    emask = jnp.broadcast_to(gsel.reshape(N_EXPERT_GROUPS, 1, tm), b3.shape).reshape(n_e, tm)
    masked = jnp.where(emask, biased, -jnp.inf)

    eidx = lax.broadcasted_iota(jnp.int32, masked.shape, 0)
    sel = jnp.zeros(masked.shape, jnp.bool_)
    top_e, top_w, hits = [], [], []
    for _ in range(TOP_K):
        mx = jnp.max(masked, axis=0, keepdims=True)
        ei = jnp.min(jnp.where(masked == mx, eidx, n_e), axis=0, keepdims=True)
        hit = eidx == ei
        top_e.append(ei)
        top_w.append(jnp.sum(jnp.where(hit, scores, 0.0), axis=0, keepdims=True))
        hits.append(hit)
        sel = sel | hit
        masked = jnp.where(hit, -jnp.inf, masked)
    wsum = functools.reduce(lambda a, b: a + b, top_w)

    selb = jnp.where(sel, 1.0, 0.0).astype(BF16)
    tri = (lax.broadcasted_iota(jnp.int32, (tm, tm), 0) < lax.broadcasted_iota(jnp.int32, (tm, tm), 1))
    ranks = _dot(selb, jnp.where(tri, 1.0, 0.0).astype(BF16)) + carry_ref[:, 0:1]
    carry_ref[...] = carry_ref[...] + jnp.sum(selb.astype(F32), axis=1, keepdims=True)

    for k in range(TOP_K):
        e_ref[k:k + 1, :] = top_e[k]
        wt_ref[k:k + 1, :] = top_w[k] / wsum * ROUTED_SCALE
        rank_ref[k:k + 1, :] = jnp.sum(jnp.where(hits[k], ranks, 0.0), axis=0, keepdims=True).astype(jnp.int32)
    cnt_ref[...] = carry_ref[...]


def _router(x, router_w_t, router_bias, tm=512):
    t, d = x.shape
    n_e = router_w_t.shape[0]
    tm = min(tm, t)
    tok = pl.BlockSpec((TOP_K, tm), lambda i: (0, i))
    e, w, rank, cnt = pl.pallas_call(
        functools.partial(_router_kernel, n_e=n_e),
        grid=(t // tm,),
        in_specs=[pl.BlockSpec((tm, d), lambda i: (i, 0)),
                  pl.BlockSpec((n_e, d), lambda i: (0, 0)),
                  pl.BlockSpec((n_e, 1), lambda i: (0, 0))],
        out_specs=[tok, tok, tok, pl.BlockSpec((n_e, HEAD_DIM), lambda i: (0, 0))],
        out_shape=[jax.ShapeDtypeStruct((TOP_K, t), jnp.int32),
                   jax.ShapeDtypeStruct((TOP_K, t), F32),
                   jax.ShapeDtypeStruct((TOP_K, t), jnp.int32),
                   jax.ShapeDtypeStruct((n_e, HEAD_DIM), F32)],
        scratch_shapes=[pltpu.VMEM((n_e, HEAD_DIM), F32)],
        compiler_params=_cparams("arbitrary"),
        name="moe_router",
    )(x, router_w_t, router_bias.reshape(n_e, 1).astype(F32))
    return e, w, rank, cnt[:, 0]


def _gather_kernel(idx_hbm, src_hbm, out_hbm, idx_smem, idx_sem, row_sem, *, chunk, n_steps):
    i = pl.program_id(0)
    n = n_steps
    slot = i % 2

    def idx_copy(step, sl):
        return pltpu.make_async_copy(idx_hbm.at[pl.ds(step * chunk, chunk)], idx_smem.at[sl], idx_sem.at[sl])

    def rows_done(step, sl):
        return pltpu.make_async_copy(src_hbm.at[pl.ds(0, chunk)], out_hbm.at[pl.ds(step * chunk, chunk)],
                                     row_sem.at[sl])

    @pl.when(i == 0)
    def _():
        idx_copy(0, 0).start()

    idx_copy(i, slot).wait()

    @pl.when(i + 1 < n)
    def _():
        idx_copy(i + 1, 1 - slot).start()

    def issue(j, carry):
        row = idx_smem[slot, j]
        pltpu.make_async_copy(src_hbm.at[row], out_hbm.at[i * chunk + j], row_sem.at[slot]).start()
        return carry

    lax.fori_loop(0, chunk, issue, 0)

    @pl.when(i > 0)
    def _():
        rows_done(i - 1, 1 - slot).wait()

    @pl.when(i == n - 1)
    def _():
        rows_done(i, slot).wait()


def _row_gather(src, idx, name="row_gather"):
    n_out = idx.shape[0]
    chunk = GATHER_CHUNK
    assert n_out % chunk == 0
    return pl.pallas_call(
        functools.partial(_gather_kernel, chunk=chunk, n_steps=n_out // chunk),
        grid=(n_out // chunk,),
        in_specs=[pl.BlockSpec(memory_space=pl.ANY), pl.BlockSpec(memory_space=pl.ANY)],
        out_specs=pl.BlockSpec(memory_space=pl.ANY),
        out_shape=jax.ShapeDtypeStruct((n_out, src.shape[1]), src.dtype),
        scratch_shapes=[pltpu.SMEM((2, chunk), jnp.int32),
                        pltpu.SemaphoreType.DMA((2,)),
                        pltpu.SemaphoreType.DMA((2,))],
        compiler_params=_cparams("arbitrary"),
        name=name,
    )(idx, src)


def _ffn_kernel(te_ref, nt_ref, x_ref, wg_ref, wu_ref, wd_ref, o_ref, wg_s, wu_s, wd_s):
    i = pl.program_id(0)
    prev = te_ref[jnp.maximum(i - 1, 0)]
    active = i < nt_ref[0]

    @pl.when(active & ((i == 0) | (te_ref[i] != prev)))
    def _():
        wg_s[...] = wg_ref[...].astype(BF16)
        wu_s[...] = wu_ref[...].astype(BF16)
        wd_s[...] = wd_ref[...].astype(BF16)

    @pl.when(active)
    def _():
        x = x_ref[...].astype(BF16)
        a = _dot(x, wg_s[...])
        u = _dot(x, wu_s[...])
        hmid = (a * jax.nn.sigmoid(a) * u).astype(BF16)
        o_ref[...] = _dot(hmid, wd_s[...]).astype(o_ref.dtype)

    @pl.when(jnp.logical_not(active))
    def _():
        o_ref[...] = jnp.zeros_like(o_ref)


def _grouped_ffn(xs, tile_expert, n_tiles, w_gate, w_up, w_down, tm, name):
    p, d = xs.shape
    hid = w_gate.shape[-1]
    return pl.pallas_call(
        _ffn_kernel,
        grid_spec=pltpu.PrefetchScalarGridSpec(
            num_scalar_prefetch=2,
            grid=(p // tm,),
            in_specs=[pl.BlockSpec((tm, d), lambda i, te, nt: (i, 0)),
                      pl.BlockSpec((None, d, hid), lambda i, te, nt: (te[i], 0, 0)),
                      pl.BlockSpec((None, d, hid), lambda i, te, nt: (te[i], 0, 0)),
                      pl.BlockSpec((None, hid, d), lambda i, te, nt: (te[i], 0, 0))],
            out_specs=pl.BlockSpec((tm, d), lambda i, te, nt: (i, 0)),
            scratch_shapes=[pltpu.VMEM((d, hid), BF16), pltpu.VMEM((d, hid), BF16), pltpu.VMEM((hid, d), BF16)],
        ),
        out_shape=jax.ShapeDtypeStruct((p, d), F32),
        compiler_params=_cparams("arbitrary"),
        name=name,
    )(tile_expert, n_tiles, xs, w_gate, w_up, w_down)


def _moe(h, router_w, router_bias, w_gate, w_up, w_down, s_gate, s_up, s_down, tm=256):
    t, d = h.shape
    n_e = router_w.shape[1]
    tm = min(tm, t)
    top_e, top_w, rank, counts = _router(h, router_w.T.astype(F32), router_bias)

    counts = counts.astype(jnp.int32)
    padded = ((counts + tm - 1) // tm) * tm
    ends = jnp.cumsum(padded)
    offsets = ends - padded
    p_rows = -(-(t * TOP_K + n_e * (tm - 1)) // GATHER_CHUNK) * GATHER_CHUNK
    n_tiles_max = p_rows // tm
    n_tiles = (ends[-1] // tm).astype(jnp.int32).reshape(1)
    tile_start = jnp.arange(n_tiles_max, dtype=jnp.int32) * tm
    tile_expert = jnp.minimum(jnp.searchsorted(ends, tile_start, side="right"), n_e - 1).astype(jnp.int32)
    dest = offsets[top_e] + rank
    tok = jnp.broadcast_to(jnp.arange(t, dtype=jnp.int32)[None, :], dest.shape)
    src_tok = jnp.zeros((p_rows,), jnp.int32).at[dest.reshape(-1)].set(tok.reshape(-1))

    xs = _row_gather(h.astype(F32), src_tok, name="moe_dispatch")
    ys = _grouped_ffn(xs, tile_expert, n_tiles, w_gate, w_up, w_down, tm, "moe_experts")
    yk = _row_gather(ys, dest.reshape(-1), name="moe_collect").reshape(TOP_K, t, d)
    shared = _grouped_ffn(h, jnp.zeros((t // tm,), jnp.int32), jnp.full((1,), t // tm, jnp.int32),
                          s_gate[None], s_up[None], s_down[None], tm, "moe_shared")
    return shared, yk, top_w.T


def kernel(x, c, w_ada, b_ada, ada_table, ln_gain, ln_bias, gqa_wqkv, gqa_wo, gqa_q_gain, gqa_k_gain, dil_wqkv, dil_wo, nat_wqkv, nat_wo, nat_rpb, dif_wqkv, dif_wo, dif_lambda, dif_subln_gain, router_w, router_bias, exp_w_gate, exp_w_up, exp_w_down, sh_w_gate, sh_w_up, sh_w_down):
    bsz, seq, d = x.shape
    depth = ada_table.shape[0]
    t = bsz * seq
    alpha = (2 * depth) ** 0.25
    n_h = d // HEAD_DIM
    n_kv = n_h // GQA_RATIO
    n_h2 = d // (2 * HEAD_DIM)
    n_kv2 = n_h2 // GQA_RATIO
    n_grp = len(DIL_PATTERNS)

    pos = jnp.arange(seq)
    cos1, sin1 = _rope_tables(pos, HEAD_DIM)
    cos1_t = jnp.concatenate([cos1, cos1], axis=-1)
    sin1_t = jnp.concatenate([-sin1, sin1], axis=-1)
    cr, sr = _rope_tables(pos // GRID_W, HEAD_DIM // 2)
    cc, sc = _rope_tables(pos % GRID_W, HEAD_DIM // 2)
    cos2_t = jnp.concatenate([cr, cr, cc, cc], axis=-1)
    sin2_t = jnp.concatenate([-sr, sr, -sc, sc], axis=-1)

    pad_rows = 16
    c_pad = jnp.zeros((pad_rows, d), F32).at[:bsz].set(c.astype(F32))
    mod_shared = _ada_proj(c_pad, w_ada, b_ada)[:bsz].reshape(bsz, N_MOD, d)

    def mod_vec(mod, j):
        return mod[:, j:j + 1, :]

    mod = mod_shared + ada_table[0]
    h = _modulate(x, mod_vec(mod, 0), mod_vec(mod, 1))
    for i in range(depth):
        kind, j = i % N_MIXERS, i // N_MIXERS
        mod = mod_shared + ada_table[i]
        h2 = h.reshape(t, d)
        if kind == 0:
            qkv = _matmul(h2, gqa_wqkv[j], F32, name="gqa_qkv")
            flags = jnp.array([3] * (n_h + n_kv) + [0] * n_kv, jnp.int32)
            gains = jnp.concatenate([jnp.broadcast_to(gqa_q_gain[j], (n_h, HEAD_DIM)),
                                     jnp.broadcast_to(gqa_k_gain[j], (n_kv, HEAD_DIM)),
                                     jnp.ones((n_kv, HEAD_DIM), F32)]).astype(F32)
            qkv = _qk_prep(qkv, flags, gains, cos2_t, sin2_t, HEAD_DIM // 4, seq)
            o = _gqa_attention(qkv, bsz, seq, n_kv)
            wo = gqa_wo[j]
        elif kind == 1:
            qkv = _matmul(h2, dil_wqkv[j], F32, name="dil_qkv")
            n_qk = n_grp * (n_h2 + n_kv2)
            flags = jnp.array([2] * n_qk + [0] * (n_grp * n_kv2), jnp.int32)
            gains = jnp.ones((n_qk + n_grp * n_kv2, HEAD_DIM), F32)
            qkv = _qk_prep(qkv, flags, gains, cos1_t, sin1_t, HEAD_DIM // 2, seq)
            outs, stats = [], []
            for grp, (window, dilation) in enumerate(DIL_PATTERNS):
                og, sg = _dilated_group(qkv, grp, n_grp, window, dilation, bsz, seq, n_kv2)
                outs.append(og)
                stats.append(sg)
            o = _dilated_merge(outs, stats, n_kv2)
            wo = dil_wo[j]
        elif kind == 2:
            qkv = _matmul(h2, nat_wqkv[j], BF16, name="nat_qkv")
            o = _nat_attention(qkv, _nat_bias_table(nat_rpb[j], n_kv), bsz, seq, n_kv)
            wo = nat_wo[j]
        else:
            qkv = _matmul(h2, dif_wqkv[j], F32, name="dif_qkv")
            n_qk = 2 * n_h2 + 2 * n_kv2
            flags = jnp.array([2] * n_qk + [0] * (2 * n_kv2), jnp.int32)
            gains = jnp.ones((n_qk + 2 * n_kv2, HEAD_DIM), F32)
            qkv = _qk_prep(qkv, flags, gains, cos1_t, sin1_t, HEAD_DIM // 2, seq)
            lam_init = 0.8 - 0.6 * math.exp(-0.3 * i)
            o = _diff_attention(qkv, dif_lambda[j], dif_subln_gain[j], lam_init, bsz, seq, n_kv2)
            wo = dif_wo[j]
        y = _matmul(o, wo, BF16, name="mixer_out").reshape(bsz, seq, d)
        x, h = _ln_residual(x, y, mod_vec(mod, 2), ln_gain[i, 0], ln_bias[i, 0], alpha,
                            shift=mod_vec(mod, 3), scale=mod_vec(mod, 4))

        shared, yk, wk = _moe(h.reshape(t, d), router_w[i], router_bias[i],
                              exp_w_gate[i], exp_w_up[i], exp_w_down[i],
                              sh_w_gate[i], sh_w_up[i], sh_w_down[i])
        if i + 1 < depth:
            nxt = mod_shared + ada_table[i + 1]
            shift_n, scale_n = mod_vec(nxt, 0), mod_vec(nxt, 1)
        else:
            shift_n = scale_n = None
        x, h = _ln_residual(x, shared.reshape(bsz, seq, d), mod_vec(mod, 5), ln_gain[i, 1], ln_bias[i, 1], alpha,
                            shift=shift_n, scale=scale_n,
                            yk=yk.reshape(TOP_K, bsz, seq, d), wk=wk.reshape(bsz, seq, TOP_K))
    return x
```

```python
import functools
import math

import jax
import jax.numpy as jnp
import numpy as np
from jax import lax
from jax.experimental import pallas as pl
from jax.experimental.pallas import tpu as pltpu

HEAD_DIM = 128
N_MIXERS = 4
GQA_RATIO = 4
GRID_W = 64
ROPE_THETA = 10000.0
LN_EPS = 1e-5
RMS_EPS = 1e-6
N_MOD = 6
DIL_PATTERNS = ((128, 1), (512, 4), (2048, 16))
NAT_WIN_ROWS = 8
NAT_WIN_COLS = 16
N_EXPERT_GROUPS = 8
TOPK_GROUPS = 4
TOP_K = 8
ROUTED_SCALE = 2.5

VMEM_LIMIT_BYTES = 56 * 1024 * 1024
NEG_BIG = -1e30
F32 = jnp.float32
BF16 = jnp.bfloat16


def _cparams(*sem):
    return pltpu.CompilerParams(dimension_semantics=("arbitrary",) * len(sem), vmem_limit_bytes=VMEM_LIMIT_BYTES)


def _tile(n, pref):
    if n <= pref:
        return n
    while n % pref:
        pref //= 2
    return pref


def _dot(a, b):
    return jnp.dot(a, b, preferred_element_type=F32)


def _dot_nt(a, b):
    return lax.dot_general(a, b, (((1,), (1,)), ((), ())), preferred_element_type=F32)


def _matmul_kernel(a_ref, w_ref, o_ref, acc_ref, *, nk):
    k = pl.program_id(2)

    @pl.when(k == 0)
    def _():
        acc_ref[...] = jnp.zeros_like(acc_ref)

    acc_ref[...] += _dot(a_ref[...].astype(BF16), w_ref[...].astype(BF16))

    @pl.when(k == nk - 1)
    def _():
        o_ref[...] = acc_ref[...].astype(o_ref.dtype)


def _matmul(a, w, out_dtype, tm=2048, tn=1024, tk=512, name="matmul"):
    m, kd = a.shape
    n = w.shape[1]
    tm, tn, tk = _tile(m, tm), _tile(n, tn), _tile(kd, tk)
    return pl.pallas_call(
        functools.partial(_matmul_kernel, nk=kd // tk),
        grid=(m // tm, n // tn, kd // tk),
        in_specs=[pl.BlockSpec((tm, tk), lambda i, j, k: (i, k)),
                  pl.BlockSpec((tk, tn), lambda i, j, k: (k, j))],
        out_specs=pl.BlockSpec((tm, tn), lambda i, j, k: (i, j)),
        out_shape=jax.ShapeDtypeStruct((m, n), out_dtype),
        scratch_shapes=[pltpu.VMEM((tm, tn), F32)],
        compiler_params=_cparams("parallel", "parallel", "arbitrary"),
        name=name,
    )(a, w)


def _ada_kernel(c_ref, w_ref, b_ref, o_ref, acc_ref, *, nk):
    k = pl.program_id(1)

    @pl.when(k == 0)
    def _():
        acc_ref[...] = jnp.zeros_like(acc_ref)

    c = c_ref[...]
    a = c * jax.nn.sigmoid(c)
    acc_ref[...] += _dot(a.astype(BF16), w_ref[...].astype(BF16))

    @pl.when(k == nk - 1)
    def _():
        o_ref[...] = acc_ref[...] + b_ref[...]


def _ada_proj(c_pad, w_ada, b_ada, tn=2048, tk=512):
    m, kd = c_pad.shape
    n = w_ada.shape[1]
    tn, tk = min(tn, n), min(tk, kd)
    return pl.pallas_call(
        functools.partial(_ada_kernel, nk=kd // tk),
        grid=(n // tn, kd // tk),
        in_specs=[pl.BlockSpec((m, tk), lambda j, k: (0, k)),
                  pl.BlockSpec((tk, tn), lambda j, k: (k, j)),
                  pl.BlockSpec((1, tn), lambda j, k: (0, j))],
        out_specs=pl.BlockSpec((m, tn), lambda j, k: (0, j)),
        out_shape=jax.ShapeDtypeStruct((m, n), F32),
        scratch_shapes=[pltpu.VMEM((m, tn), F32)],
        compiler_params=_cparams("parallel", "arbitrary"),
        name="ada_proj",
    )(c_pad, w_ada, b_ada.reshape(1, n))


def _modulate_kernel(x_ref, shift_ref, scale_ref, h_ref):
    h_ref[...] = (x_ref[...] * (1.0 + scale_ref[...]) + shift_ref[...]).astype(h_ref.dtype)


def _modulate(x, shift, scale, ts=512):
    b, s, d = x.shape
    ts = min(ts, s)
    row = pl.BlockSpec((None, ts, d), lambda bi, i: (bi, i, 0))
    vec = pl.BlockSpec((None, 1, d), lambda bi, i: (bi, 0, 0))
    return pl.pallas_call(
        _modulate_kernel,
        grid=(b, s // ts),
        in_specs=[row, vec, vec],
        out_specs=row,
        out_shape=jax.ShapeDtypeStruct((b, s, d), BF16),
        compiler_params=_cparams("parallel", "parallel"),
        name="modulate",
    )(x, shift, scale)


def _ln_kernel(x_ref, y_ref, gate_ref, g_ref, b_ref, shift_ref, scale_ref, xo_ref, h_ref, h3_ref, *, alpha, nc, pitch):
    ts = x_ref.shape[0]
    z = alpha * x_ref[...] + gate_ref[...] * y_ref[...].astype(F32)
    zc = z - jnp.mean(z, axis=-1, keepdims=True)
    var = jnp.mean(zc * zc, axis=-1, keepdims=True)
    xn = zc * lax.rsqrt(var + LN_EPS) * g_ref[...] + b_ref[...]
    xo_ref[...] = xn
    hf = xn * (1.0 + scale_ref[...]) + shift_ref[...]
    h_ref[...] = hf.astype(h_ref.dtype)
    _slab_store(h3_ref, hf, ts, nc, pitch)
    for r in range(nc, pitch):
        h3_ref[pl.ds(r, ts, stride=pitch), :] = jnp.zeros((ts, HEAD_DIM), F32)


def _ln_residual(x, y, gate, g, b, alpha, shift, scale, nc, pitch, ts=128):
    bsz, s, d = x.shape
    ts = _tile(s, ts)
    nsb = s // ts
    row = pl.BlockSpec((None, ts, d), lambda bi, i: (bi, i, 0))
    vec = pl.BlockSpec((None, 1, d), lambda bi, i: (bi, 0, 0))
    par = pl.BlockSpec((1, d), lambda bi, i: (0, 0))
    return pl.pallas_call(
        functools.partial(_ln_kernel, alpha=alpha, nc=nc, pitch=pitch),
        grid=(bsz, nsb),
        in_specs=[row, row, vec, par, par, vec, vec],
        out_specs=[row, row, pl.BlockSpec((ts * pitch, HEAD_DIM), lambda bi, i: (bi * nsb + i, 0))],
        out_shape=[jax.ShapeDtypeStruct((bsz, s, d), F32),
                   jax.ShapeDtypeStruct((bsz, s, d), BF16),
                   jax.ShapeDtypeStruct((bsz * s * pitch, HEAD_DIM), F32)],
        compiler_params=_cparams("arbitrary", "arbitrary"),
        name="ln_mixer",
    )(x, y, gate, g.reshape(1, d), b.reshape(1, d), shift, scale)


def _prep_kernel(x_ref, gain_ref, cos_ref, sin_ref, o_ref, *, flags, half):
    cos = cos_ref[...]
    sin = sin_ref[...]
    lane = lax.broadcasted_iota(jnp.int32, cos.shape, 1)
    first_half = (lane % (2 * half)) < half
    for j, flag in enumerate(flags):
        sl = slice(j * HEAD_DIM, (j + 1) * HEAD_DIM)
        x = x_ref[:, sl].astype(F32)
        if flag & 1:
            x = x * lax.rsqrt(jnp.mean(x * x, axis=-1, keepdims=True) + RMS_EPS) * gain_ref[j:j + 1, :]
        if flag & 2:
            if 2 * half == HEAD_DIM:
                partner = pltpu.roll(x, half, 1)
            else:
                up = pltpu.roll(x, HEAD_DIM - half, 1)
                dn = pltpu.roll(x, half, 1)
                partner = jnp.where(first_half, up, dn)
            x = x * cos + partner * sin
        o_ref[:, sl] = x.astype(o_ref.dtype)


def _qk_prep(qkv, flags, gains, cos_t, sin_t, half, seq, ts=256):
    t, n = qkv.shape
    nc = n // HEAD_DIM
    ts = _tile(seq, ts)
    nsb = seq // ts
    return pl.pallas_call(
        functools.partial(_prep_kernel, flags=tuple(flags), half=half),
        grid=(t // ts,),
        in_specs=[pl.BlockSpec((ts, n), lambda i: (i, 0)),
                  pl.BlockSpec((nc, HEAD_DIM), lambda i: (0, 0)),
                  pl.BlockSpec((ts, HEAD_DIM), lambda i: (i % nsb, 0)),
                  pl.BlockSpec((ts, HEAD_DIM), lambda i: (i % nsb, 0))],
        out_specs=pl.BlockSpec((ts, n), lambda i: (i, 0)),
        out_shape=jax.ShapeDtypeStruct((t, n), BF16),
        compiler_params=_cparams("arbitrary"),
        name="qk_prep",
    )(qkv, gains, cos_t, sin_t)


def _rope_tables(pos, dim):
    inv = ROPE_THETA ** (-jnp.arange(0, dim, 2, dtype=F32) / dim)
    ang = pos.astype(F32)[:, None] * inv[None, :]
    return jnp.cos(ang), jnp.sin(ang)


def _softmax_pv(s, v):
    m = jnp.max(s, axis=-1, keepdims=True)
    p = jnp.exp(s - m)
    l = jnp.sum(p, axis=-1, keepdims=True)
    return _dot(p.astype(BF16), v) / l


def _gqa_attn_kernel(q_ref, k_ref, v_ref, o_ref, *, n_q, scale):
    k = k_ref[...]
    v = v_ref[...]
    for j in range(n_q):
        sl = slice(j * HEAD_DIM, (j + 1) * HEAD_DIM)
        s = _dot_nt(q_ref[:, sl], k) * scale
        o_ref[:, sl] = _softmax_pv(s, v).astype(o_ref.dtype)


def _diff_attn_kernel(q_ref, k_ref, v_ref, lam_ref, gain_ref, o_ref, *, n_g, scale, lam_init):
    lp = lam_ref[...]
    lam = (jnp.exp(jnp.sum(lp[0:1] * lp[1:2], axis=-1, keepdims=True))
           - jnp.exp(jnp.sum(lp[2:3] * lp[3:4], axis=-1, keepdims=True)) + lam_init)
    v = v_ref[...]
    dv = v.shape[-1]
    for g in range(n_g):
        o = None
        for c in range(2):
            j = 2 * g + c
            q = q_ref[:, j * HEAD_DIM:(j + 1) * HEAD_DIM]
            s = _dot_nt(q, k_ref[:, c * HEAD_DIM:(c + 1) * HEAD_DIM]) * scale
            oc = _softmax_pv(s, v)
            o = oc if c == 0 else o - lam * oc
        o = o * lax.rsqrt(jnp.mean(o * o, axis=-1, keepdims=True) + RMS_EPS) * gain_ref[...]
        o_ref[:, g * dv:(g + 1) * dv] = (o * (1.0 - lam_init)).astype(o_ref.dtype)


def _gqa_attention(qkv, bsz, seq, n_kv, tq=256):
    g = GQA_RATIO
    n_h = n_kv * g
    tq = min(tq, seq)
    nq = seq // tq
    qw = g * HEAD_DIM
    return pl.pallas_call(
        functools.partial(_gqa_attn_kernel, n_q=g, scale=HEAD_DIM ** -0.5),
        grid=(bsz, n_kv, nq),
        in_specs=[pl.BlockSpec((tq, qw), lambda b, h, i: (b * nq + i, h)),
                  pl.BlockSpec((seq, HEAD_DIM), lambda b, h, i: (b, n_h + h)),
                  pl.BlockSpec((seq, HEAD_DIM), lambda b, h, i: (b, n_h + n_kv + h))],
        out_specs=pl.BlockSpec((tq, qw), lambda b, h, i: (b * nq + i, h)),
        out_shape=jax.ShapeDtypeStruct((bsz * seq, n_h * HEAD_DIM), BF16),
        compiler_params=_cparams("parallel", "parallel", "arbitrary"),
        name="gqa_attention",
    )(qkv, qkv, qkv)


def _diff_attention(qkv, lam_params, sub_gain, lam_init, bsz, seq, n_kv, tq=256):
    g = GQA_RATIO
    n_h = n_kv * g
    tq = min(tq, seq)
    nq = seq // tq
    qw = g * 2 * HEAD_DIM
    dv = 2 * HEAD_DIM
    k_blk0 = (2 * n_h * HEAD_DIM) // dv
    v_blk0 = k_blk0 + n_kv
    return pl.pallas_call(
        functools.partial(_diff_attn_kernel, n_g=g, scale=HEAD_DIM ** -0.5, lam_init=lam_init),
        grid=(bsz, n_kv, nq),
        in_specs=[pl.BlockSpec((tq, qw), lambda b, h, i: (b * nq + i, h)),
                  pl.BlockSpec((seq, dv), lambda b, h, i: (b, k_blk0 + h)),
                  pl.BlockSpec((seq, dv), lambda b, h, i: (b, v_blk0 + h)),
                  pl.BlockSpec((4, HEAD_DIM), lambda b, h, i: (0, 0)),
                  pl.BlockSpec((1, dv), lambda b, h, i: (0, 0))],
        out_specs=pl.BlockSpec((tq, g * dv), lambda b, h, i: (b * nq + i, h)),
        out_shape=jax.ShapeDtypeStruct((bsz * seq, n_h * dv), BF16),
        compiler_params=_cparams("parallel", "parallel", "arbitrary"),
        name="diff_attention",
    )(qkv, qkv, qkv, lam_params.astype(F32), sub_gain.reshape(1, dv).astype(F32))


def _dil_attn_kernel(q_ref, k_ref, v_ref, o_ref, st_ref, *, n_g, half, sub, scale):
    tl = q_ref.shape[0]
    length = k_ref.shape[0]
    win = min(length, sub + 2 * half)
    i = pl.program_id(3)
    lane = lax.broadcasted_iota(jnp.int32, (sub, HEAD_DIM), 1)
    for sb in range(tl // sub):
        qs = i * tl + sb * sub
        start = jnp.clip(qs - half, 0, length - win)
        start = pl.multiple_of(start, half)
        kw = k_ref[pl.ds(start, win), :]
        vw = v_ref[pl.ds(start, win), :]
        kpos = start + lax.broadcasted_iota(jnp.int32, (sub, win), 1)
        qpos = qs + lax.broadcasted_iota(jnp.int32, (sub, win), 0)
        valid = jnp.abs(kpos - qpos) <= half
        stats = jnp.zeros((sub, HEAD_DIM), F32)
        for g in range(n_g):
            sl = slice(g * HEAD_DIM, (g + 1) * HEAD_DIM)
            s = _dot_nt(q_ref[sb * sub:(sb + 1) * sub, sl], kw) * scale
            s = jnp.where(valid, s, NEG_BIG)
            m = jnp.max(s, axis=-1, keepdims=True)
            p = jnp.exp(s - m)
            l = jnp.sum(p, axis=-1, keepdims=True)
            o_ref[sb * sub:(sb + 1) * sub, sl] = (_dot(p.astype(BF16), vw) / l).astype(o_ref.dtype)
            stats = jnp.where(lane == g, m, stats)
            stats = jnp.where(lane == n_g + g, l, stats)
        st_ref[sb * sub:(sb + 1) * sub, :] = stats


def _dilated_group(qkv, grp, n_grp, window, dilation, bsz, seq, n_kv, tl=512, sub=128):
    g = GQA_RATIO
    n_h = n_kv * g
    d = dilation
    length = seq // d
    half = window // (2 * d)
    sub = min(sub, length)
    tl = min(tl, length)
    nl = length // tl
    n_cols = qkv.shape[1]
    qkv_v = qkv.reshape(bsz * length, d * n_cols)
    qw = g * HEAD_DIM
    assert n_cols % qw == 0 and length % tl == 0 and tl % sub == 0
    q_blk = lambda b, r, h, i: (b * nl + i, (r * n_cols + grp * n_h * HEAD_DIM) // qw + h)
    k_col0 = n_grp * n_h * HEAD_DIM + grp * n_kv * HEAD_DIM
    v_col0 = n_grp * (n_h + n_kv) * HEAD_DIM + grp * n_kv * HEAD_DIM
    k_blk = lambda b, r, h, i: (b, (r * n_cols + k_col0) // HEAD_DIM + h)
    v_blk = lambda b, r, h, i: (b, (r * n_cols + v_col0) // HEAD_DIM + h)
    o, st = pl.pallas_call(
        functools.partial(_dil_attn_kernel, n_g=g, half=half, sub=sub, scale=HEAD_DIM ** -0.5),
        grid=(bsz, d, n_kv, nl),
        in_specs=[pl.BlockSpec((tl, qw), q_blk),
                  pl.BlockSpec((length, HEAD_DIM), k_blk),
                  pl.BlockSpec((length, HEAD_DIM), v_blk)],
        out_specs=[pl.BlockSpec((tl, qw), lambda b, r, h, i: (b * nl + i, r * n_kv + h)),
                   pl.BlockSpec((tl, HEAD_DIM), lambda b, r, h, i: (b * nl + i, r * n_kv + h))],
        out_shape=[jax.ShapeDtypeStruct((bsz * length, d * n_h * HEAD_DIM), F32),
                   jax.ShapeDtypeStruct((bsz * length, d * n_kv * HEAD_DIM), F32)],
        compiler_params=_cparams("parallel", "parallel", "parallel", "arbitrary"),
        name=f"dilated_attention_d{d}",
    )(qkv_v, qkv_v, qkv_v)
    return o.reshape(bsz * seq, n_h * HEAD_DIM), st.reshape(bsz * seq, n_kv * HEAD_DIM)


def _dil_merge_kernel(*refs, n_grp, n_kv, n_g):
    o_refs, st_refs, out_ref = refs[:n_grp], refs[n_grp:2 * n_grp], refs[2 * n_grp]
    for kv in range(n_kv):
        st = [r[:, kv * HEAD_DIM:(kv + 1) * HEAD_DIM] for r in st_refs]
        for g in range(n_g):
            h = kv * n_g + g
            ms = [s[:, g:g + 1] for s in st]
            ls = [s[:, n_g + g:n_g + g + 1] for s in st]
            mmax = functools.reduce(jnp.maximum, ms)
            ws = [l * jnp.exp(m - mmax) for m, l in zip(ms, ls)]
            wsum = functools.reduce(lambda a, b: a + b, ws)
            sl = slice(h * HEAD_DIM, (h + 1) * HEAD_DIM)
            acc = None
            for w, o_ref in zip(ws, o_refs):
                term = (w / wsum) * o_ref[:, sl]
                acc = term if acc is None else acc + term
            out_ref[:, sl] = acc.astype(out_ref.dtype)


def _dilated_merge(outs, stats, n_kv, ts=256):
    t, n = outs[0].shape
    ts = min(ts, t)
    n_grp = len(outs)
    o_spec = pl.BlockSpec((ts, n), lambda i: (i, 0))
    s_spec = pl.BlockSpec((ts, stats[0].shape[1]), lambda i: (i, 0))
    return pl.pallas_call(
        functools.partial(_dil_merge_kernel, n_grp=n_grp, n_kv=n_kv, n_g=GQA_RATIO),
        grid=(t // ts,),
        in_specs=[o_spec] * n_grp + [s_spec] * n_grp,
        out_specs=o_spec,
        out_shape=jax.ShapeDtypeStruct((t, n), BF16),
        compiler_params=_cparams("parallel"),
        name="dilated_merge",
    )(*outs, *stats)


def _nat_attn_kernel(q_ref, k_ref, v_ref, bias_ref, o_ref, *, n_g, rows, scale):
    rows_per_step = q_ref.shape[0] // GRID_W
    i = pl.program_id(2)
    nk = NAT_WIN_ROWS * GRID_W
    for rr in range(rows_per_step):
        r = i * rows_per_step + rr
        rs = jnp.clip(r - NAT_WIN_ROWS // 2, 0, rows - NAT_WIN_ROWS)
        ro = r - rs
        start = pl.multiple_of(rs * GRID_W, GRID_W)
        kw = k_ref[pl.ds(start, nk), :]
        vw = v_ref[pl.ds(start, nk), :]
        rsl = slice(rr * GRID_W, (rr + 1) * GRID_W)
        q = jnp.concatenate([q_ref[rsl, g * HEAD_DIM:(g + 1) * HEAD_DIM] for g in range(n_g)], axis=0)
        s = _dot_nt(q, kw) * scale + bias_ref[ro]
        o = _softmax_pv(s, vw)
        for g in range(n_g):
            o_ref[rsl, g * HEAD_DIM:(g + 1) * HEAD_DIM] = o[g * GRID_W:(g + 1) * GRID_W].astype(o_ref.dtype)


def _nat_bias_table(rpb, n_kv):
    n_h = rpb.shape[0]
    g = n_h // n_kv
    nr, ncol, w = NAT_WIN_ROWS, NAT_WIN_COLS, GRID_W
    ro, ki = np.meshgrid(np.arange(nr), np.arange(nr), indexing="ij")
    rowsel = np.zeros((nr, nr, 2 * nr - 1), np.float32)
    rowsel[ro, ki, ki - ro + nr - 1] = 1.0
    c, kc = np.meshgrid(np.arange(w), np.arange(w), indexing="ij")
    cs = np.clip(c - ncol // 2, 0, w - ncol)
    valid = (kc >= cs) & (kc < cs + ncol)
    colsel = np.zeros((2 * ncol - 1, w, w), np.float32)
    colsel[(kc - c + ncol - 1)[valid], c[valid], kc[valid]] = 1.0
    rows = jnp.einsum("hij,rki->hrkj", rpb.astype(F32), rowsel, precision=lax.Precision.HIGHEST)
    tab = jnp.einsum("hrkj,jcq->hrckq", rows, colsel, precision=lax.Precision.HIGHEST)
    tab = jnp.where(valid[None, None, :, None, :], tab, NEG_BIG)
    tab = tab.reshape(n_kv, g, nr, w, nr * w)
    return tab.transpose(0, 2, 1, 3, 4).reshape(n_kv, nr, g * w, nr * w)


def _nat_attention(qkv, bias_tab, bsz, seq, n_kv, rows_per_step=8):
    g = GQA_RATIO
    n_h = n_kv * g
    rows = seq // GRID_W
    assert rows >= NAT_WIN_ROWS
    rows_per_step = min(rows_per_step, rows)
    tq = rows_per_step * GRID_W
    nq = seq // tq
    qw = g * HEAD_DIM
    return pl.pallas_call(
        functools.partial(_nat_attn_kernel, n_g=g, rows=rows, scale=HEAD_DIM ** -0.5),
        grid=(bsz, n_kv, nq),
        in_specs=[pl.BlockSpec((tq, qw), lambda b, h, i: (b * nq + i, h)),
                  pl.BlockSpec((seq, HEAD_DIM), lambda b, h, i: (b, n_h + h)),
                  pl.BlockSpec((seq, HEAD_DIM), lambda b, h, i: (b, n_h + n_kv + h)),
                  pl.BlockSpec((None,) + bias_tab.shape[1:], lambda b, h, i: (h, 0, 0, 0))],
        out_specs=pl.BlockSpec((tq, qw), lambda b, h, i: (b * nq + i, h)),
        out_shape=jax.ShapeDtypeStruct((bsz * seq, n_h * HEAD_DIM), BF16),
        compiler_params=_cparams("parallel", "parallel", "arbitrary"),
        name="nat_attention",
    )(qkv, qkv, qkv, bias_tab)


def _router_kernel(h_ref, w_ref, bias_ref, e_ref, wt_ref, rank_ref, cnt_ref, carry_ref, *, n_e):
    i = pl.program_id(0)
    tm = h_ref.shape[0]
    per_group = n_e // N_EXPERT_GROUPS

    @pl.when(i == 0)
    def _():
        carry_ref[...] = jnp.zeros_like(carry_ref)

    h = h_ref[...]
    w = w_ref[...]
    w_hi = w.astype(BF16)
    w_lo = (w - w_hi.astype(F32)).astype(BF16)
    logits = _dot_nt(w_hi, h) + _dot_nt(w_lo, h)
    scores = jax.nn.sigmoid(logits)
    biased = scores + bias_ref[...]

    b3 = biased.reshape(N_EXPERT_GROUPS, per_group, tm)
    idx3 = lax.broadcasted_iota(jnp.int32, b3.shape, 1)
    m1 = jnp.max(b3, axis=1, keepdims=True)
    i1 = jnp.min(jnp.where(b3 == m1, idx3, per_group), axis=1, keepdims=True)
    m2 = jnp.max(jnp.where(idx3 == i1, -jnp.inf, b3), axis=1, keepdims=True)
    gs = (m1 + m2).reshape(N_EXPERT_GROUPS, tm)

    gidx = lax.broadcasted_iota(jnp.int32, gs.shape, 0)
    gsel = jnp.zeros(gs.shape, jnp.bool_)
    for _ in range(TOPK_GROUPS):
        gm = jnp.max(gs, axis=0, keepdims=True)
        gi = jnp.min(jnp.where(gs == gm, gidx, N_EXPERT_GROUPS), axis=0, keepdims=True)
        hit = gidx == gi
        gsel = gsel | hit
        gs = jnp.where(hit, -jnp.inf, gs)
    emask = jnp.broadcast_to(gsel.reshape(N_EXPERT_GROUPS, 1, tm), b3.shape).reshape(n_e, tm)
    masked = jnp.where(emask, biased, -jnp.inf)

    eidx = lax.broadcasted_iota(jnp.int32, masked.shape, 0)
    sel = jnp.zeros(masked.shape, jnp.bool_)
    top_e, top_w, hits = [], [], []
    for _ in range(TOP_K):
        mx = jnp.max(masked, axis=0, keepdims=True)
        ei = jnp.min(jnp.where(masked == mx, eidx, n_e), axis=0, keepdims=True)
        hit = eidx == ei
        top_e.append(ei)
        top_w.append(jnp.sum(jnp.where(hit, scores, 0.0), axis=0, keepdims=True))
        hits.append(hit)
        sel = sel | hit
        masked = jnp.where(hit, -jnp.inf, masked)
    wsum = functools.reduce(lambda a, b: a + b, top_w)

    selb = jnp.where(sel, 1.0, 0.0).astype(BF16)
    tri = (lax.broadcasted_iota(jnp.int32, (tm, tm), 0) < lax.broadcasted_iota(jnp.int32, (tm, tm), 1))
    ranks = _dot(selb, jnp.where(tri, 1.0, 0.0).astype(BF16)) + carry_ref[:, 0:1]
    carry_ref[...] = carry_ref[...] + jnp.sum(selb.astype(F32), axis=1, keepdims=True)

    for k in range(TOP_K):
        e_ref[k:k + 1, :] = top_e[k]
        wt_ref[k:k + 1, :] = top_w[k] / wsum * ROUTED_SCALE
        rank_ref[k:k + 1, :] = jnp.sum(jnp.where(hits[k], ranks, 0.0), axis=0, keepdims=True).astype(jnp.int32)
    cnt_ref[...] = carry_ref[...]


def _router(x, router_w_t, router_bias, tm=512):
    t, d = x.shape
    n_e = router_w_t.shape[0]
    tm = min(tm, t)
    tok = pl.BlockSpec((TOP_K, tm), lambda i: (0, i))
    e, w, rank, cnt = pl.pallas_call(
        functools.partial(_router_kernel, n_e=n_e),
        grid=(t // tm,),
        in_specs=[pl.BlockSpec((tm, d), lambda i: (i, 0)),
                  pl.BlockSpec((n_e, d), lambda i: (0, 0)),
                  pl.BlockSpec((n_e, 1), lambda i: (0, 0))],
        out_specs=[tok, tok, tok, pl.BlockSpec((n_e, HEAD_DIM), lambda i: (0, 0))],
        out_shape=[jax.ShapeDtypeStruct((TOP_K, t), jnp.int32),
                   jax.ShapeDtypeStruct((TOP_K, t), F32),
                   jax.ShapeDtypeStruct((TOP_K, t), jnp.int32),
                   jax.ShapeDtypeStruct((n_e, HEAD_DIM), F32)],
        scratch_shapes=[pltpu.VMEM((n_e, HEAD_DIM), F32)],
        compiler_params=_cparams("arbitrary"),
        name="moe_router",
    )(x, router_w_t, router_bias.reshape(n_e, 1).astype(F32))
    return e, w, rank, cnt[:, 0]


def _slab_pitch(nc):
    pitch = nc + 8
    return pitch if (pitch // 8) % 2 else pitch + 8


def _slab_load(ref, n_tok, nc, pitch):
    return jnp.concatenate([ref[pl.ds(c, n_tok, stride=pitch), :] for c in range(nc)], axis=1)


def _slab_store(ref, val, n_tok, nc, pitch):
    for c in range(nc):
        ref[pl.ds(c, n_tok, stride=pitch), :] = val[:, c * HEAD_DIM:(c + 1) * HEAD_DIM]


def _gather_pipeline(i, n_act, idx_hbm, src_hbm, idx_smem, buf, idx_sem, g_sem, *, n_idx, nc, pitch):
    slot = i % 2
    rows = n_idx * nc

    def idx_copy(tile, sl):
        return pltpu.make_async_copy(idx_hbm.at[pl.ds(tile * n_idx, n_idx)], idx_smem.at[sl], idx_sem.at[sl])

    def issue(sl):
        def body(j, carry):
            row = idx_smem[sl, j]
            pltpu.make_async_copy(src_hbm.at[pl.ds(pl.multiple_of(row * pitch, 8), nc), :],
                                  buf.at[sl, pl.ds(pl.multiple_of(j * pitch, 8), nc), :], g_sem.at[sl]).start()
            return carry
        lax.fori_loop(0, n_idx, body, 0)

    @pl.when(i == 0)
    def _():
        idx_copy(0, 0).start()
        idx_copy(0, 0).wait()
        issue(0)

        @pl.when(1 < n_act)
        def _():
            idx_copy(1, 1).start()

    @pl.when(i + 1 < n_act)
    def _():
        idx_copy(i + 1, 1 - slot).wait()
        issue(1 - slot)

    @pl.when(i + 2 < n_act)
    def _():
        idx_copy(i + 2, slot).start()

    pltpu.make_async_copy(src_hbm.at[pl.ds(0, rows), :], buf.at[slot, pl.ds(0, rows), :], g_sem.at[slot]).wait()


def _experts_kernel(te_ref, nt_ref, idx_hbm, h3_hbm, wg_ref, wu_ref, wd_ref, ys_ref,
                    idx_smem, xbuf, idx_sem, g_sem, *, tm, nc, pitch):
    i = pl.program_id(0)
    nt = nt_ref[0]

    @pl.when(i < nt)
    def _():
        _gather_pipeline(i, nt, idx_hbm, h3_hbm, idx_smem, xbuf, idx_sem, g_sem, n_idx=tm, nc=nc, pitch=pitch)
        x = _slab_load(xbuf.at[i % 2], tm, nc, pitch).astype(BF16)
        a = _dot(x, wg_ref[...])
        u = _dot(x, wu_ref[...])
        hmid = (a * jax.nn.sigmoid(a) * u).astype(BF16)
        y = _dot(hmid, wd_ref[...])
        _slab_store(ys_ref, y, tm, nc, pitch)
        for r in range(nc, pitch):
            ys_ref[pl.ds(r, tm, stride=pitch), :] = jnp.zeros((tm, HEAD_DIM), F32)

    @pl.when(i >= nt)
    def _():
        ys_ref[...] = jnp.zeros_like(ys_ref)


def _routed_experts(h3, src_tok, tile_expert, n_tiles, w_gate, w_up, w_down, tm, nc, pitch):
    n_tiles_max = tile_expert.shape[0]
    d = nc * HEAD_DIM
    hid = w_gate.shape[-1]
    return pl.pallas_call(
        functools.partial(_experts_kernel, tm=tm, nc=nc, pitch=pitch),
        grid_spec=pltpu.PrefetchScalarGridSpec(
            num_scalar_prefetch=2,
            grid=(n_tiles_max,),
            in_specs=[pl.BlockSpec(memory_space=pl.ANY),
                      pl.BlockSpec(memory_space=pl.ANY),
                      pl.BlockSpec((None, d, hid), lambda i, te, nt: (te[i], 0, 0)),
                      pl.BlockSpec((None, d, hid), lambda i, te, nt: (te[i], 0, 0)),
                      pl.BlockSpec((None, hid, d), lambda i, te, nt: (te[i], 0, 0))],
            out_specs=pl.BlockSpec((tm * pitch, HEAD_DIM), lambda i, te, nt: (i, 0)),
            scratch_shapes=[pltpu.SMEM((2, tm), jnp.int32),
                            pltpu.VMEM((2, tm * pitch, HEAD_DIM), F32),
                            pltpu.SemaphoreType.DMA((2,)),
                            pltpu.SemaphoreType.DMA((2,))],
        ),
        out_shape=jax.ShapeDtypeStruct((n_tiles_max * tm * pitch, HEAD_DIM), F32),
        compiler_params=_cparams("arbitrary"),
        name="moe_experts",
    )(tile_expert, n_tiles, src_tok, h3, w_gate, w_up, w_down)


def _ffn_kernel(x_ref, wg_ref, wu_ref, wd_ref, o_ref):
    x = x_ref[...]
    a = _dot(x, wg_ref[...])
    u = _dot(x, wu_ref[...])
    hmid = (a * jax.nn.sigmoid(a) * u).astype(BF16)
    o_ref[...] = _dot(hmid, wd_ref[...]).astype(o_ref.dtype)


def _shared_ffn(h, w_gate, w_up, w_down, tm=512):
    t, d = h.shape
    hid = w_gate.shape[-1]
    tm = _tile(t, tm)
    return pl.pallas_call(
        _ffn_kernel,
        grid=(t // tm,),
        in_specs=[pl.BlockSpec((tm, d), lambda i: (i, 0)),
                  pl.BlockSpec((d, hid), lambda i: (0, 0)),
                  pl.BlockSpec((d, hid), lambda i: (0, 0)),
                  pl.BlockSpec((hid, d), lambda i: (0, 0))],
        out_specs=pl.BlockSpec((tm, d), lambda i: (i, 0)),
        out_shape=jax.ShapeDtypeStruct((t, d), BF16),
        compiler_params=_cparams("arbitrary"),
        name="moe_shared",
    )(h, w_gate, w_up, w_down)


def _combine_ln_kernel(*refs, alpha, ts, nc, pitch, n_steps, with_h):
    it = iter(refs)
    idx_hbm, ys_hbm, x_ref, sh_ref, wk_ref, gate_ref, g_ref, b_ref = [next(it) for _ in range(8)]
    if with_h:
        shift_ref, scale_ref = next(it), next(it)
    xo_ref = next(it)
    if with_h:
        h_ref = next(it)
    idx_smem, buf, idx_sem, g_sem = [next(it) for _ in range(4)]

    i = pl.program_id(0)
    _gather_pipeline(i, n_steps, idx_hbm, ys_hbm, idx_smem, buf, idx_sem, g_sem,
                     n_idx=TOP_K * ts, nc=nc, pitch=pitch)
    slabs = buf.at[i % 2]
    wk = wk_ref[...]
    wkb = [jnp.broadcast_to(wk[:, k:k + 1], (ts, HEAD_DIM)) for k in range(TOP_K)]
    pieces = []
    for c in range(nc):
        acc = None
        for k in range(TOP_K):
            term = wkb[k] * slabs[pl.ds(k * ts * pitch + c, ts, stride=pitch), :]
            acc = term if acc is None else acc + term
        pieces.append(acc)
    y = sh_ref[...].astype(F32) + jnp.concatenate(pieces, axis=1)
    z = alpha * x_ref[...] + gate_ref[...] * y
    zc = z - jnp.mean(z, axis=-1, keepdims=True)
    var = jnp.mean(zc * zc, axis=-1, keepdims=True)
    xn = zc * lax.rsqrt(var + LN_EPS) * g_ref[...] + b_ref[...]
    xo_ref[...] = xn
    if with_h:
        h_ref[...] = (xn * (1.0 + scale_ref[...]) + shift_ref[...]).astype(h_ref.dtype)


def _combine_ln(x, shared, ys, dest, wk, gate, g, b, alpha, nc, pitch, shift=None, scale=None, ts=64):
    bsz, s, d = x.shape
    t = bsz * s
    ts = _tile(s, ts)
    nsb = s // ts
    n_steps = t // ts
    with_h = shift is not None
    idx = dest.reshape(TOP_K, n_steps, ts).transpose(1, 0, 2).reshape(-1)
    row = pl.BlockSpec((ts, d), lambda i: (i, 0))
    vec = pl.BlockSpec((None, 1, d), lambda i: (i // nsb, 0, 0))
    par = pl.BlockSpec((1, d), lambda i: (0, 0))
    hbm = pl.BlockSpec(memory_space=pl.ANY)
    args = [idx, ys, x.reshape(t, d), shared, wk, gate, g.reshape(1, d), b.reshape(1, d)]
    specs = [hbm, hbm, row, row, pl.BlockSpec((ts, TOP_K), lambda i: (i, 0)), vec, par, par]
    if with_h:
        args += [shift, scale]
        specs += [vec, vec]
    out_shape = [jax.ShapeDtypeStruct((t, d), F32)]
    out_specs = [row]
    if with_h:
        out_shape.append(jax.ShapeDtypeStruct((t, d), BF16))
        out_specs.append(row)
    outs = pl.pallas_call(
        functools.partial(_combine_ln_kernel, alpha=alpha, ts=ts, nc=nc, pitch=pitch, n_steps=n_steps, with_h=with_h),
        grid=(n_steps,),
        in_specs=specs,
        out_specs=out_specs,
        out_shape=out_shape,
        scratch_shapes=[pltpu.SMEM((2, TOP_K * ts), jnp.int32),
                        pltpu.VMEM((2, TOP_K * ts * pitch, HEAD_DIM), F32),
                        pltpu.SemaphoreType.DMA((2,)),
                        pltpu.SemaphoreType.DMA((2,))],
        compiler_params=_cparams("arbitrary"),
        name="moe_combine_ln",
    )(*args)
    x_new = outs[0].reshape(bsz, s, d)
    return (x_new, outs[1].reshape(bsz, s, d)) if with_h else (x_new, None)


def _moe(h, h3, router_w, router_bias, w_gate, w_up, w_down, s_gate, s_up, s_down, nc, pitch, tm=256):
    t, d = h.shape
    n_e = router_w.shape[1]
    tm = min(tm, t)
    top_e, top_w, rank, counts = _router(h, router_w.T.astype(F32), router_bias)

    counts = counts.astype(jnp.int32)
    padded = ((counts + tm - 1) // tm) * tm
    ends = jnp.cumsum(padded)
    offsets = ends - padded
    n_tiles_max = (t * TOP_K + n_e * (tm - 1)) // tm
    n_tiles = (ends[-1] // tm).astype(jnp.int32).reshape(1)
    tile_start = jnp.arange(n_tiles_max, dtype=jnp.int32) * tm
    tile_expert = jnp.minimum(jnp.sum((ends[None, :] <= tile_start[:, None]).astype(jnp.int32), axis=1), n_e - 1)
    onehot = top_e[:, :, None] == jnp.arange(n_e, dtype=jnp.int32)[None, None, :]
    dest = rank + jnp.sum(jnp.where(onehot, offsets[None, None, :], 0), axis=-1)
    tok = jnp.broadcast_to(jnp.arange(t, dtype=jnp.int32)[None, :], dest.shape)
    src_tok = jnp.zeros((n_tiles_max * tm,), jnp.int32).at[dest.reshape(-1)].set(
        tok.reshape(-1), unique_indices=True)

    ys = _routed_experts(h3, src_tok, tile_expert, n_tiles, w_gate.astype(BF16), w_up.astype(BF16),
                         w_down.astype(BF16), tm, nc, pitch)
    shared = _shared_ffn(h, s_gate.astype(BF16), s_up.astype(BF16), s_down.astype(BF16))
    return shared, ys, dest, top_w.T


def kernel(x, c, w_ada, b_ada, ada_table, ln_gain, ln_bias, gqa_wqkv, gqa_wo, gqa_q_gain, gqa_k_gain, dil_wqkv, dil_wo, nat_wqkv, nat_wo, nat_rpb, dif_wqkv, dif_wo, dif_lambda, dif_subln_gain, router_w, router_bias, exp_w_gate, exp_w_up, exp_w_down, sh_w_gate, sh_w_up, sh_w_down):
    bsz, seq, d = x.shape
    depth = ada_table.shape[0]
    t = bsz * seq
    alpha = (2 * depth) ** 0.25
    n_h = d // HEAD_DIM
    n_kv = n_h // GQA_RATIO
    n_h2 = d // (2 * HEAD_DIM)
    n_kv2 = n_h2 // GQA_RATIO
    n_grp = len(DIL_PATTERNS)
    nc = d // HEAD_DIM
    pitch = _slab_pitch(nc)

    pos = jnp.arange(seq)
    cos1, sin1 = _rope_tables(pos, HEAD_DIM)
    cos1_t = jnp.concatenate([cos1, cos1], axis=-1)
    sin1_t = jnp.concatenate([-sin1, sin1], axis=-1)
    cr, sr = _rope_tables(pos // GRID_W, HEAD_DIM // 2)
    cc, sc = _rope_tables(pos % GRID_W, HEAD_DIM // 2)
    cos2_t = jnp.concatenate([cr, cr, cc, cc], axis=-1)
    sin2_t = jnp.concatenate([-sr, sr, -sc, sc], axis=-1)

    pad_rows = 16
    c_pad = jnp.zeros((pad_rows, d), F32).at[:bsz].set(c.astype(F32))
    mod_shared = _ada_proj(c_pad, w_ada, b_ada)[:bsz].reshape(bsz, N_MOD, d)

    def mod_vec(mod, j):
        return mod[:, j:j + 1, :]

    mod = mod_shared + ada_table[0]
    h = _modulate(x, mod_vec(mod, 0), mod_vec(mod, 1))
    for i in range(depth):
        kind, j = i % N_MIXERS, i // N_MIXERS
        mod = mod_shared + ada_table[i]
        h2 = h.reshape(t, d)
        if kind == 0:
            qkv = _matmul(h2, gqa_wqkv[j], F32, name="gqa_qkv")
            flags = [3] * (n_h + n_kv) + [0] * n_kv
            gains = jnp.concatenate([jnp.broadcast_to(gqa_q_gain[j], (n_h, HEAD_DIM)),
                                     jnp.broadcast_to(gqa_k_gain[j], (n_kv, HEAD_DIM)),
                                     jnp.ones((n_kv, HEAD_DIM), F32)]).astype(F32)
            qkv = _qk_prep(qkv, flags, gains, cos2_t, sin2_t, HEAD_DIM // 4, seq)
            o = _gqa_attention(qkv, bsz, seq, n_kv)
            wo = gqa_wo[j]
        elif kind == 1:
            qkv = _matmul(h2, dil_wqkv[j], F32, name="dil_qkv")
            n_qk = n_grp * (n_h2 + n_kv2)
            flags = [2] * n_qk + [0] * (n_grp * n_kv2)
            gains = jnp.ones((n_qk + n_grp * n_kv2, HEAD_DIM), F32)
            qkv = _qk_prep(qkv, flags, gains, cos1_t, sin1_t, HEAD_DIM // 2, seq)
            outs, stats = [], []
            for grp, (window, dilation) in enumerate(DIL_PATTERNS):
                og, sg = _dilated_group(qkv, grp, n_grp, window, dilation, bsz, seq, n_kv2)
                outs.append(og)
                stats.append(sg)
            o = _dilated_merge(outs, stats, n_kv2)
            wo = dil_wo[j]
        elif kind == 2:
            qkv = _matmul(h2, nat_wqkv[j], BF16, name="nat_qkv")
            o = _nat_attention(qkv, _nat_bias_table(nat_rpb[j], n_kv), bsz, seq, n_kv)
            wo = nat_wo[j]
        else:
            qkv = _matmul(h2, dif_wqkv[j], F32, name="dif_qkv")
            n_qk = 2 * n_h2 + 2 * n_kv2
            flags = [2] * n_qk + [0] * (2 * n_kv2)
            gains = jnp.ones((n_qk + 2 * n_kv2, HEAD_DIM), F32)
            qkv = _qk_prep(qkv, flags, gains, cos1_t, sin1_t, HEAD_DIM // 2, seq)
            lam_init = 0.8 - 0.6 * math.exp(-0.3 * i)
            o = _diff_attention(qkv, dif_lambda[j], dif_subln_gain[j], lam_init, bsz, seq, n_kv2)
            wo = dif_wo[j]
        y = _matmul(o, wo, BF16, name="mixer_out").reshape(bsz, seq, d)
        x, h, h3 = _ln_residual(x, y, mod_vec(mod, 2), ln_gain[i, 0], ln_bias[i, 0], alpha,
                                mod_vec(mod, 3), mod_vec(mod, 4), nc, pitch)

        shared, ys, dest, wk = _moe(h.reshape(t, d), h3, router_w[i], router_bias[i],
                                    exp_w_gate[i], exp_w_up[i], exp_w_down[i],
                                    sh_w_gate[i], sh_w_up[i], sh_w_down[i], nc, pitch)
        if i + 1 < depth:
            nxt = mod_shared + ada_table[i + 1]
            shift_n, scale_n = mod_vec(nxt, 0), mod_vec(nxt, 1)
        else:
            shift_n = scale_n = None
        x, h = _combine_ln(x, shared, ys, dest, wk, mod_vec(mod, 5), ln_gain[i, 1], ln_bias[i, 1], alpha,
                           nc, pitch, shift=shift_n, scale=scale_n)
    return x
```

```python
import functools
import math

import jax
import jax.numpy as jnp
import numpy as np
from jax import lax
from jax.experimental import pallas as pl
from jax.experimental.pallas import tpu as pltpu

HEAD_DIM = 128
N_MIXERS = 4
GQA_RATIO = 4
GRID_W = 64
ROPE_THETA = 10000.0
LN_EPS = 1e-5
RMS_EPS = 1e-6
N_MOD = 6
DIL_PATTERNS = ((128, 1), (512, 4), (2048, 16))
NAT_WIN_ROWS = 8
NAT_WIN_COLS = 16
N_EXPERT_GROUPS = 8
TOPK_GROUPS = 4
TOP_K = 8
ROUTED_SCALE = 2.5

VMEM_LIMIT_BYTES = 56 * 1024 * 1024
NEG_BIG = -1e30

F32 = jnp.float32
BF16 = jnp.bfloat16


def _cparams(*sem):
    return pltpu.CompilerParams(dimension_semantics=("arbitrary",) * len(sem), vmem_limit_bytes=VMEM_LIMIT_BYTES)


def _tile(n, pref):
    if n <= pref:
        return n
    while n % pref:
        pref //= 2
    return pref


def _dot(a, b):
    return jnp.dot(a, b, preferred_element_type=F32)


def _dot_nt(a, b):
    return lax.dot_general(a, b, (((1,), (1,)), ((), ())), preferred_element_type=F32)


def _matmul_kernel(a_ref, w_ref, o_ref, acc_ref, *, nk):
    k = pl.program_id(2)

    @pl.when(k == 0)
    def _():
        acc_ref[...] = jnp.zeros_like(acc_ref)

    acc_ref[...] += _dot(a_ref[...].astype(BF16), w_ref[...].astype(BF16))

    @pl.when(k == nk - 1)
    def _():
        o_ref[...] = acc_ref[...].astype(o_ref.dtype)


def _matmul(a, w, out_dtype, tm=2048, tn=1024, tk=512, name="matmul"):
    m, kd = a.shape
    n = w.shape[1]
    tm, tn, tk = _tile(m, tm), _tile(n, tn), _tile(kd, tk)
    return pl.pallas_call(
        functools.partial(_matmul_kernel, nk=kd // tk),
        grid=(m // tm, n // tn, kd // tk),
        in_specs=[pl.BlockSpec((tm, tk), lambda i, j, k: (i, k)),
                  pl.BlockSpec((tk, tn), lambda i, j, k: (k, j))],
        out_specs=pl.BlockSpec((tm, tn), lambda i, j, k: (i, j)),
        out_shape=jax.ShapeDtypeStruct((m, n), out_dtype),
        scratch_shapes=[pltpu.VMEM((tm, tn), F32)],
        compiler_params=_cparams("parallel", "parallel", "arbitrary"),
        name=name,
    )(a, w)


def _ada_kernel(c_ref, w_ref, b_ref, o_ref, acc_ref, *, nk):
    k = pl.program_id(1)

    @pl.when(k == 0)
    def _():
        acc_ref[...] = jnp.zeros_like(acc_ref)

    c = c_ref[...]
    a = c * jax.nn.sigmoid(c)
    acc_ref[...] += _dot(a.astype(BF16), w_ref[...].astype(BF16))

    @pl.when(k == nk - 1)
    def _():
        o_ref[...] = acc_ref[...] + b_ref[...]


def _ada_proj(c_pad, w_ada, b_ada, tn=2048, tk=512):
    m, kd = c_pad.shape
    n = w_ada.shape[1]
    tn, tk = min(tn, n), min(tk, kd)
    return pl.pallas_call(
        functools.partial(_ada_kernel, nk=kd // tk),
        grid=(n // tn, kd // tk),
        in_specs=[pl.BlockSpec((m, tk), lambda j, k: (0, k)),
                  pl.BlockSpec((tk, tn), lambda j, k: (k, j)),
                  pl.BlockSpec((1, tn), lambda j, k: (0, j))],
        out_specs=pl.BlockSpec((m, tn), lambda j, k: (0, j)),
        out_shape=jax.ShapeDtypeStruct((m, n), F32),
        scratch_shapes=[pltpu.VMEM((m, tn), F32)],
        compiler_params=_cparams("parallel", "arbitrary"),
        name="ada_proj",
    )(c_pad, w_ada, b_ada.reshape(1, n))


def _modulate_kernel(x_ref, shift_ref, scale_ref, h_ref):
    h_ref[...] = (x_ref[...] * (1.0 + scale_ref[...]) + shift_ref[...]).astype(h_ref.dtype)


def _modulate(x, shift, scale, ts=512):
    b, s, d = x.shape
    ts = min(ts, s)
    row = pl.BlockSpec((None, ts, d), lambda bi, i: (bi, i, 0))
    vec = pl.BlockSpec((None, 1, d), lambda bi, i: (bi, 0, 0))
    return pl.pallas_call(
        _modulate_kernel,
        grid=(b, s // ts),
        in_specs=[row, vec, vec],
        out_specs=row,
        out_shape=jax.ShapeDtypeStruct((b, s, d), BF16),
        compiler_params=_cparams("parallel", "parallel"),
        name="modulate",
    )(x, shift, scale)


def _ln_kernel(x_ref, y_ref, gate_ref, g_ref, b_ref, shift_ref, scale_ref, xo_ref, h_ref, h3_ref, *, alpha, nc, pitch):
    ts = x_ref.shape[0]
    z = alpha * x_ref[...] + gate_ref[...] * y_ref[...].astype(F32)
    zc = z - jnp.mean(z, axis=-1, keepdims=True)
    var = jnp.mean(zc * zc, axis=-1, keepdims=True)
    xn = zc * lax.rsqrt(var + LN_EPS) * g_ref[...] + b_ref[...]
    xo_ref[...] = xn
    hf = xn * (1.0 + scale_ref[...]) + shift_ref[...]
    h_ref[...] = hf.astype(h_ref.dtype)
    _slab_store(h3_ref, hf, ts, nc, pitch)
    for r in range(nc, pitch):
        h3_ref[pl.ds(r, ts, stride=pitch), :] = jnp.zeros((ts, HEAD_DIM), F32)


def _ln_residual(x, y, gate, g, b, alpha, shift, scale, nc, pitch, ts=128):
    bsz, s, d = x.shape
    ts = _tile(s, ts)
    nsb = s // ts
    row = pl.BlockSpec((None, ts, d), lambda bi, i: (bi, i, 0))
    vec = pl.BlockSpec((None, 1, d), lambda bi, i: (bi, 0, 0))
    par = pl.BlockSpec((1, d), lambda bi, i: (0, 0))
    return pl.pallas_call(
        functools.partial(_ln_kernel, alpha=alpha, nc=nc, pitch=pitch),
        grid=(bsz, nsb),
        in_specs=[row, row, vec, par, par, vec, vec],
        out_specs=[row, row, pl.BlockSpec((ts * pitch, HEAD_DIM), lambda bi, i: (bi * nsb + i, 0))],
        out_shape=[jax.ShapeDtypeStruct((bsz, s, d), F32),
                   jax.ShapeDtypeStruct((bsz, s, d), BF16),
                   jax.ShapeDtypeStruct((bsz * s * pitch, HEAD_DIM), F32)],
        compiler_params=_cparams("arbitrary", "arbitrary"),
        name="ln_mixer",
    )(x, y, gate, g.reshape(1, d), b.reshape(1, d), shift, scale)


def _prep_kernel(x_ref, gain_ref, cos_ref, sin_ref, o_ref, *, flags, half):
    cos = cos_ref[...]
    sin = sin_ref[...]
    lane = lax.broadcasted_iota(jnp.int32, cos.shape, 1)
    first_half = (lane % (2 * half)) < half
    for j, flag in enumerate(flags):
        sl = slice(j * HEAD_DIM, (j + 1) * HEAD_DIM)
        x = x_ref[:, sl].astype(F32)
        if flag & 1:
            x = x * lax.rsqrt(jnp.mean(x * x, axis=-1, keepdims=True) + RMS_EPS) * gain_ref[j:j + 1, :]
        if flag & 2:
            if 2 * half == HEAD_DIM:
                partner = pltpu.roll(x, half, 1)
            else:
                up = pltpu.roll(x, HEAD_DIM - half, 1)
                dn = pltpu.roll(x, half, 1)
                partner = jnp.where(first_half, up, dn)
            x = x * cos + partner * sin
        o_ref[:, sl] = x.astype(o_ref.dtype)


def _qk_prep(qkv, flags, gains, cos_t, sin_t, half, seq, ts=256):
    t, n = qkv.shape
    nc = n // HEAD_DIM
    ts = _tile(seq, ts)
    nsb = seq // ts
    return pl.pallas_call(
        functools.partial(_prep_kernel, flags=tuple(flags), half=half),
        grid=(t // ts,),
        in_specs=[pl.BlockSpec((ts, n), lambda i: (i, 0)),
                  pl.BlockSpec((nc, HEAD_DIM), lambda i: (0, 0)),
                  pl.BlockSpec((ts, HEAD_DIM), lambda i: (i % nsb, 0)),
                  pl.BlockSpec((ts, HEAD_DIM), lambda i: (i % nsb, 0))],
        out_specs=pl.BlockSpec((ts, n), lambda i: (i, 0)),
        out_shape=jax.ShapeDtypeStruct((t, n), BF16),
        compiler_params=_cparams("arbitrary"),
        name="qk_prep",
    )(qkv, gains, cos_t, sin_t)


def _rope_tables(pos, dim):
    inv = ROPE_THETA ** (-jnp.arange(0, dim, 2, dtype=F32) / dim)
    ang = pos.astype(F32)[:, None] * inv[None, :]
    return jnp.cos(ang), jnp.sin(ang)


def _softmax_pv(s, v):
    m = jnp.max(s, axis=-1, keepdims=True)
    p = jnp.exp(s - m)
    l = jnp.sum(p, axis=-1, keepdims=True)
    return _dot(p.astype(BF16), v) / l


def _dense_head(q, k, v, scale):
    return _softmax_pv(_dot_nt(q, k) * scale, v)


def _gqa_attn_kernel(q_ref, k_ref, v_ref, o_ref, *, n_q, scale):
    k = k_ref[...]
    v = v_ref[...]
    for j in range(n_q):
        sl = slice(j * HEAD_DIM, (j + 1) * HEAD_DIM)
        o_ref[:, sl] = _dense_head(q_ref[:, sl], k, v, scale).astype(o_ref.dtype)


def _diff_attn_kernel(q_ref, k_ref, v_ref, lam_ref, gain_ref, o_ref, *, n_g, scale, lam_init):
    lp = lam_ref[...]
    lam = (jnp.exp(jnp.sum(lp[0:1] * lp[1:2], axis=-1, keepdims=True))
           - jnp.exp(jnp.sum(lp[2:3] * lp[3:4], axis=-1, keepdims=True)) + lam_init)
    v = v_ref[...]
    dv = v.shape[-1]
    for g in range(n_g):
        o = None
        for c in range(2):
            j = 2 * g + c
            q = q_ref[:, j * HEAD_DIM:(j + 1) * HEAD_DIM]
            oc = _dense_head(q, k_ref[:, c * HEAD_DIM:(c + 1) * HEAD_DIM], v, scale)
            o = oc if c == 0 else o - lam * oc
        o = o * lax.rsqrt(jnp.mean(o * o, axis=-1, keepdims=True) + RMS_EPS) * gain_ref[...]
        o_ref[:, g * dv:(g + 1) * dv] = (o * (1.0 - lam_init)).astype(o_ref.dtype)


def _gqa_attention(qkv, bsz, seq, n_kv, tq=256):
    g = GQA_RATIO
    n_h = n_kv * g
    tq = min(tq, seq)
    nq = seq // tq
    qw = g * HEAD_DIM
    return pl.pallas_call(
        functools.partial(_gqa_attn_kernel, n_q=g, scale=HEAD_DIM ** -0.5),
        grid=(bsz, n_kv, nq),
        in_specs=[pl.BlockSpec((tq, qw), lambda b, h, i: (b * nq + i, h)),
                  pl.BlockSpec((seq, HEAD_DIM), lambda b, h, i: (b, n_h + h)),
                  pl.BlockSpec((seq, HEAD_DIM), lambda b, h, i: (b, n_h + n_kv + h))],
        out_specs=pl.BlockSpec((tq, qw), lambda b, h, i: (b * nq + i, h)),
        out_shape=jax.ShapeDtypeStruct((bsz * seq, n_h * HEAD_DIM), BF16),
        compiler_params=_cparams("parallel", "parallel", "arbitrary"),
        name="gqa_attention",
    )(qkv, qkv, qkv)


def _diff_attention(qkv, lam_params, sub_gain, lam_init, bsz, seq, n_kv, tq=256):
    g = GQA_RATIO
    n_h = n_kv * g
    tq = min(tq, seq)
    nq = seq // tq
    qw = g * 2 * HEAD_DIM
    dv = 2 * HEAD_DIM
    k_blk0 = (2 * n_h * HEAD_DIM) // dv
    v_blk0 = k_blk0 + n_kv
    return pl.pallas_call(
        functools.partial(_diff_attn_kernel, n_g=g, scale=HEAD_DIM ** -0.5, lam_init=lam_init),
        grid=(bsz, n_kv, nq),
        in_specs=[pl.BlockSpec((tq, qw), lambda b, h, i: (b * nq + i, h)),
                  pl.BlockSpec((seq, dv), lambda b, h, i: (b, k_blk0 + h)),
                  pl.BlockSpec((seq, dv), lambda b, h, i: (b, v_blk0 + h)),
                  pl.BlockSpec((4, HEAD_DIM), lambda b, h, i: (0, 0)),
                  pl.BlockSpec((1, dv), lambda b, h, i: (0, 0))],
        out_specs=pl.BlockSpec((tq, g * dv), lambda b, h, i: (b * nq + i, h)),
        out_shape=jax.ShapeDtypeStruct((bsz * seq, n_h * dv), BF16),
        compiler_params=_cparams("parallel", "parallel", "arbitrary"),
        name="diff_attention",
    )(qkv, qkv, qkv, lam_params.astype(F32), sub_gain.reshape(1, dv).astype(F32))


def _dil_attn_kernel(q_ref, k_ref, v_ref, o_ref, st_ref, *, n_g, half, sub, scale):
    tl = q_ref.shape[0]
    length = k_ref.shape[0]
    win = min(length, sub + 2 * half)
    i = pl.program_id(3)
    lane = lax.broadcasted_iota(jnp.int32, (sub, HEAD_DIM), 1)
    for sb in range(tl // sub):
        qs = i * tl + sb * sub
        start = jnp.clip(qs - half, 0, length - win)
        start = pl.multiple_of(start, half)
        kw = k_ref[pl.ds(start, win), :]
        vw = v_ref[pl.ds(start, win), :]
        kpos = start + lax.broadcasted_iota(jnp.int32, (sub, win), 1)
        qpos = qs + lax.broadcasted_iota(jnp.int32, (sub, win), 0)
        valid = jnp.abs(kpos - qpos) <= half
        stats = jnp.zeros((sub, HEAD_DIM), F32)
        for g in range(n_g):
            sl = slice(g * HEAD_DIM, (g + 1) * HEAD_DIM)
            s = _dot_nt(q_ref[sb * sub:(sb + 1) * sub, sl], kw) * scale
            s = jnp.where(valid, s, NEG_BIG)
            m = jnp.max(s, axis=-1, keepdims=True)
            p = jnp.exp(s - m)
            l = jnp.sum(p, axis=-1, keepdims=True)
            o_ref[sb * sub:(sb + 1) * sub, sl] = (_dot(p.astype(BF16), vw) / l).astype(o_ref.dtype)
            stats = jnp.where(lane == g, m, stats)
            stats = jnp.where(lane == n_g + g, l, stats)
        st_ref[sb * sub:(sb + 1) * sub, :] = stats


def _dilated_group(qkv, grp, n_grp, window, dilation, bsz, seq, n_kv, tl=512, sub=128):
    g = GQA_RATIO
    n_h = n_kv * g
    d = dilation
    length = seq // d
    half = window // (2 * d)
    sub = min(sub, length)
    tl = min(tl, length)
    nl = length // tl
    n_cols = qkv.shape[1]
    qkv_v = qkv.reshape(bsz * length, d * n_cols)
    qw = g * HEAD_DIM
    assert n_cols % qw == 0 and length % tl == 0 and tl % sub == 0
    q_blk = lambda b, r, h, i: (b * nl + i, (r * n_cols + grp * n_h * HEAD_DIM) // qw + h)
    k_col0 = n_grp * n_h * HEAD_DIM + grp * n_kv * HEAD_DIM
    v_col0 = n_grp * (n_h + n_kv) * HEAD_DIM + grp * n_kv * HEAD_DIM
    k_blk = lambda b, r, h, i: (b, (r * n_cols + k_col0) // HEAD_DIM + h)
    v_blk = lambda b, r, h, i: (b, (r * n_cols + v_col0) // HEAD_DIM + h)
    o, st = pl.pallas_call(
        functools.partial(_dil_attn_kernel, n_g=g, half=half, sub=sub, scale=HEAD_DIM ** -0.5),
        grid=(bsz, d, n_kv, nl),
        in_specs=[pl.BlockSpec((tl, qw), q_blk),
                  pl.BlockSpec((length, HEAD_DIM), k_blk),
                  pl.BlockSpec((length, HEAD_DIM), v_blk)],
        out_specs=[pl.BlockSpec((tl, qw), lambda b, r, h, i: (b * nl + i, r * n_kv + h)),
                   pl.BlockSpec((tl, HEAD_DIM), lambda b, r, h, i: (b * nl + i, r * n_kv + h))],
        out_shape=[jax.ShapeDtypeStruct((bsz * length, d * n_h * HEAD_DIM), F32),
                   jax.ShapeDtypeStruct((bsz * length, d * n_kv * HEAD_DIM), F32)],
        compiler_params=_cparams("parallel", "parallel", "parallel", "arbitrary"),
        name=f"dilated_attention_d{d}",
    )(qkv_v, qkv_v, qkv_v)
    return o.reshape(bsz * seq, n_h * HEAD_DIM), st.reshape(bsz * seq, n_kv * HEAD_DIM)


def _dil_merge_kernel(*refs, n_grp, n_kv, n_g):
    o_refs, st_refs, out_ref = refs[:n_grp], refs[n_grp:2 * n_grp], refs[2 * n_grp]
    for kv in range(n_kv):
        st = [r[:, kv * HEAD_DIM:(kv + 1) * HEAD_DIM] for r in st_refs]
        for g in range(n_g):
            h = kv * n_g + g
            ms = [s[:, g:g + 1] for s in st]
            ls = [s[:, n_g + g:n_g + g + 1] for s in st]
            mmax = functools.reduce(jnp.maximum, ms)
            ws = [l * jnp.exp(m - mmax) for m, l in zip(ms, ls)]
            wsum = functools.reduce(lambda a, b: a + b, ws)
            sl = slice(h * HEAD_DIM, (h + 1) * HEAD_DIM)
            acc = None
            for w, o_ref in zip(ws, o_refs):
                term = (w / wsum) * o_ref[:, sl]
                acc = term if acc is None else acc + term
            out_ref[:, sl] = acc.astype(out_ref.dtype)


def _dilated_merge(outs, stats, n_kv, ts=256):
    t, n = outs[0].shape
    ts = min(ts, t)
    n_grp = len(outs)
    o_spec = pl.BlockSpec((ts, n), lambda i: (i, 0))
    s_spec = pl.BlockSpec((ts, stats[0].shape[1]), lambda i: (i, 0))
    return pl.pallas_call(
        functools.partial(_dil_merge_kernel, n_grp=n_grp, n_kv=n_kv, n_g=GQA_RATIO),
        grid=(t // ts,),
        in_specs=[o_spec] * n_grp + [s_spec] * n_grp,
        out_specs=o_spec,
        out_shape=jax.ShapeDtypeStruct((t, n), BF16),
        compiler_params=_cparams("parallel"),
        name="dilated_merge",
    )(*outs, *stats)


def _nat_attn_kernel(q_ref, k_ref, v_ref, bias_ref, o_ref, *, n_g, rows, scale):
    rows_per_step = q_ref.shape[0] // GRID_W
    i = pl.program_id(2)
    nk = NAT_WIN_ROWS * GRID_W
    for rr in range(rows_per_step):
        r = i * rows_per_step + rr
        rs = jnp.clip(r - NAT_WIN_ROWS // 2, 0, rows - NAT_WIN_ROWS)
        ro = r - rs
        start = pl.multiple_of(rs * GRID_W, GRID_W)
        kw = k_ref[pl.ds(start, nk), :]
        vw = v_ref[pl.ds(start, nk), :]
        rsl = slice(rr * GRID_W, (rr + 1) * GRID_W)
        q = jnp.concatenate([q_ref[rsl, g * HEAD_DIM:(g + 1) * HEAD_DIM] for g in range(n_g)], axis=0)
        s = _dot_nt(q, kw) * scale + bias_ref[ro]
        o = _softmax_pv(s, vw)
        for g in range(n_g):
            o_ref[rsl, g * HEAD_DIM:(g + 1) * HEAD_DIM] = o[g * GRID_W:(g + 1) * GRID_W].astype(o_ref.dtype)


def _nat_bias_table(rpb, n_kv):
    n_h = rpb.shape[0]
    g = n_h // n_kv
    nr, ncol, w = NAT_WIN_ROWS, NAT_WIN_COLS, GRID_W
    ro, ki = np.meshgrid(np.arange(nr), np.arange(nr), indexing="ij")
    rowsel = np.zeros((nr, nr, 2 * nr - 1), np.float32)
    rowsel[ro, ki, ki - ro + nr - 1] = 1.0
    c, kc = np.meshgrid(np.arange(w), np.arange(w), indexing="ij")
    cs = np.clip(c - ncol // 2, 0, w - ncol)
    valid = (kc >= cs) & (kc < cs + ncol)
    colsel = np.zeros((2 * ncol - 1, w, w), np.float32)
    colsel[(kc - c + ncol - 1)[valid], c[valid], kc[valid]] = 1.0
    rows = jnp.einsum("hij,rki->hrkj", rpb.astype(F32), rowsel, precision=lax.Precision.HIGHEST)
    tab = jnp.einsum("hrkj,jcq->hrckq", rows, colsel, precision=lax.Precision.HIGHEST)
    tab = jnp.where(valid[None, None, :, None, :], tab, NEG_BIG)
    tab = tab.reshape(n_kv, g, nr, w, nr * w)
    return tab.transpose(0, 2, 1, 3, 4).reshape(n_kv, nr, g * w, nr * w)


def _nat_attention(qkv, bias_tab, bsz, seq, n_kv, rows_per_step=8):
    g = GQA_RATIO
    n_h = n_kv * g
    rows = seq // GRID_W
    assert rows >= NAT_WIN_ROWS
    rows_per_step = min(rows_per_step, rows)
    tq = rows_per_step * GRID_W
    nq = seq // tq
    qw = g * HEAD_DIM
    return pl.pallas_call(
        functools.partial(_nat_attn_kernel, n_g=g, rows=rows, scale=HEAD_DIM ** -0.5),
        grid=(bsz, n_kv, nq),
        in_specs=[pl.BlockSpec((tq, qw), lambda b, h, i: (b * nq + i, h)),
                  pl.BlockSpec((seq, HEAD_DIM), lambda b, h, i: (b, n_h + h)),
                  pl.BlockSpec((seq, HEAD_DIM), lambda b, h, i: (b, n_h + n_kv + h)),
                  pl.BlockSpec((None,) + bias_tab.shape[1:], lambda b, h, i: (h, 0, 0, 0))],
        out_specs=pl.BlockSpec((tq, qw), lambda b, h, i: (b * nq + i, h)),
        out_shape=jax.ShapeDtypeStruct((bsz * seq, n_h * HEAD_DIM), BF16),
        compiler_params=_cparams("parallel", "parallel", "arbitrary"),
        name="nat_attention",
    )(qkv, qkv, qkv, bias_tab)


def _router_kernel(h_ref, w_ref, bias_ref, e_ref, wt_ref, rank_ref, cnt_ref, carry_ref, *, n_e):
    i = pl.program_id(0)
    tm = h_ref.shape[0]
    per_group = n_e // N_EXPERT_GROUPS

    @pl.when(i == 0)
    def _():
        carry_ref[...] = jnp.zeros_like(carry_ref)

    h = h_ref[...]
    w = w_ref[...]
    w_hi = w.astype(BF16)
    w_lo = (w - w_hi.astype(F32)).astype(BF16)
    logits = _dot_nt(w_hi, h) + _dot_nt(w_lo, h)
    scores = jax.nn.sigmoid(logits)
    biased = scores + bias_ref[...]

    b3 = biased.reshape(N_EXPERT_GROUPS, per_group, tm)
    idx3 = lax.broadcasted_iota(jnp.int32, b3.shape, 1)
    m1 = jnp.max(b3, axis=1, keepdims=True)
    i1 = jnp.min(jnp.where(b3 == m1, idx3, per_group), axis=1, keepdims=True)
    m2 = jnp.max(jnp.where(idx3 == i1, -jnp.inf, b3), axis=1, keepdims=True)
    gs = (m1 + m2).reshape(N_EXPERT_GROUPS, tm)

    gidx = lax.broadcasted_iota(jnp.int32, gs.shape, 0)
    gsel = jnp.zeros(gs.shape, jnp.bool_)
    for _ in range(TOPK_GROUPS):
        gm = jnp.max(gs, axis=0, keepdims=True)
        gi = jnp.min(jnp.where(gs == gm, gidx, N_EXPERT_GROUPS), axis=0, keepdims=True)
        hit = gidx == gi
        gsel = gsel | hit
        gs = jnp.where(hit, -jnp.inf, gs)
    emask = jnp.broadcast_to(gsel.reshape(N_EXPERT_GROUPS, 1, tm), b3.shape).reshape(n_e, tm)
    masked = jnp.where(emask, biased, -jnp.inf)

    eidx = lax.broadcasted_iota(jnp.int32, masked.shape, 0)
    sel = jnp.zeros(masked.shape, jnp.bool_)
    top_e, top_w, hits = [], [], []
    for _ in range(TOP_K):
        mx = jnp.max(masked, axis=0, keepdims=True)
        ei = jnp.min(jnp.where(masked == mx, eidx, n_e), axis=0, keepdims=True)
        hit = eidx == ei
        top_e.append(ei)
        top_w.append(jnp.sum(jnp.where(hit, scores, 0.0), axis=0, keepdims=True))
        hits.append(hit)
        sel = sel | hit
        masked = jnp.where(hit, -jnp.inf, masked)
    wsum = functools.reduce(lambda a, b: a + b, top_w)

    selb = jnp.where(sel, 1.0, 0.0).astype(BF16)
    tri = (lax.broadcasted_iota(jnp.int32, (tm, tm), 0) < lax.broadcasted_iota(jnp.int32, (tm, tm), 1))
    ranks = _dot(selb, jnp.where(tri, 1.0, 0.0).astype(BF16)) + carry_ref[:, 0:1]
    carry_ref[...] = carry_ref[...] + jnp.sum(selb.astype(F32), axis=1, keepdims=True)

    for k in range(TOP_K):
        e_ref[k:k + 1, :] = top_e[k]
        wt_ref[k:k + 1, :] = top_w[k] / wsum * ROUTED_SCALE
        rank_ref[k:k + 1, :] = jnp.sum(jnp.where(hits[k], ranks, 0.0), axis=0, keepdims=True).astype(jnp.int32)
    cnt_ref[...] = carry_ref[...]


def _router(x, router_w_t, router_bias, tm=512):
    t, d = x.shape
    n_e = router_w_t.shape[0]
    tm = min(tm, t)
    tok = pl.BlockSpec((TOP_K, tm), lambda i: (0, i))
    e, w, rank, cnt = pl.pallas_call(
        functools.partial(_router_kernel, n_e=n_e),
        grid=(t // tm,),
        in_specs=[pl.BlockSpec((tm, d), lambda i: (i, 0)),
                  pl.BlockSpec((n_e, d), lambda i: (0, 0)),
                  pl.BlockSpec((n_e, 1), lambda i: (0, 0))],
        out_specs=[tok, tok, tok, pl.BlockSpec((n_e, HEAD_DIM), lambda i: (0, 0))],
        out_shape=[jax.ShapeDtypeStruct((TOP_K, t), jnp.int32),
                   jax.ShapeDtypeStruct((TOP_K, t), F32),
                   jax.ShapeDtypeStruct((TOP_K, t), jnp.int32),
                   jax.ShapeDtypeStruct((n_e, HEAD_DIM), F32)],
        scratch_shapes=[pltpu.VMEM((n_e, HEAD_DIM), F32)],
        compiler_params=_cparams("arbitrary"),
        name="moe_router",
    )(x, router_w_t, router_bias.reshape(n_e, 1).astype(F32))
    return e, w, rank, cnt[:, 0]


def _slab_pitch(nc):
    pitch = nc + 8
    return pitch if (pitch // 8) % 2 else pitch + 8


def _slab_load(ref, n_tok, nc, pitch):
    return jnp.concatenate([ref[pl.ds(c, n_tok, stride=pitch), :] for c in range(nc)], axis=1)


def _slab_store(ref, val, n_tok, nc, pitch):
    for c in range(nc):
        ref[pl.ds(c, n_tok, stride=pitch), :] = val[:, c * HEAD_DIM:(c + 1) * HEAD_DIM]


def _gather_pipeline(i, n_act, idx_hbm, src_hbm, idx_smem, buf, idx_sem, g_sem, *, n_idx, nc, pitch):
    slot = i % 2
    rows = n_idx * nc

    def idx_copy(tile, sl):
        return pltpu.make_async_copy(idx_hbm.at[pl.ds(tile * n_idx, n_idx)],
                                     idx_smem.at[pl.ds(sl * n_idx, n_idx)], idx_sem.at[sl])

    def issue(sl):
        def body(j, carry):
            row0 = idx_smem[sl * n_idx + j]
            pltpu.make_async_copy(src_hbm.at[pl.ds(pl.multiple_of(row0, 8), nc), :],
                                  buf.at[sl, pl.ds(pl.multiple_of(j * pitch, 8), nc), :], g_sem.at[sl]).start()
            return carry
        lax.fori_loop(0, n_idx, body, 0, unroll=8)

    @pl.when(i == 0)
    def _():
        idx_copy(0, 0).start()
        idx_copy(0, 0).wait()
        issue(0)

        @pl.when(1 < n_act)
        def _():
            idx_copy(1, 1).start()

    @pl.when(i + 1 < n_act)
    def _():
        idx_copy(i + 1, 1 - slot).wait()
        issue(1 - slot)

    @pl.when(i + 2 < n_act)
    def _():
        idx_copy(i + 2, slot).start()

    pltpu.make_async_copy(src_hbm.at[pl.ds(0, rows), :], buf.at[slot, pl.ds(0, rows), :], g_sem.at[slot]).wait()


def _experts_kernel(te_ref, nt_ref, idx_hbm, h3_hbm, wg_ref, wu_ref, wd_ref, ys_ref,
                    idx_smem, xbuf, idx_sem, g_sem, *, tm, nc, pitch):
    i = pl.program_id(0)
    nt = nt_ref[0]

    @pl.when(i < nt)
    def _():
        _gather_pipeline(i, nt, idx_hbm, h3_hbm, idx_smem, xbuf, idx_sem, g_sem, n_idx=tm, nc=nc, pitch=pitch)
        x = _slab_load(xbuf.at[i % 2], tm, nc, pitch).astype(BF16)
        a = _dot(x, wg_ref[...])
        u = _dot(x, wu_ref[...])
        hmid = (a * jax.nn.sigmoid(a) * u).astype(BF16)
        y = _dot(hmid, wd_ref[...])
        _slab_store(ys_ref, y, tm, nc, pitch)
        for r in range(nc, pitch):
            ys_ref[pl.ds(r, tm, stride=pitch), :] = jnp.zeros((tm, HEAD_DIM), F32)

    @pl.when(i >= nt)
    def _():
        ys_ref[...] = jnp.zeros_like(ys_ref)


def _routed_experts(h3, src_tok, tile_expert, n_tiles, w_gate, w_up, w_down, layer, tm, nc, pitch):
    n_tiles_max = tile_expert.shape[0]
    d = nc * HEAD_DIM
    hid = w_gate.shape[-1]
    return pl.pallas_call(
        functools.partial(_experts_kernel, tm=tm, nc=nc, pitch=pitch),
        grid_spec=pltpu.PrefetchScalarGridSpec(
            num_scalar_prefetch=2,
            grid=(n_tiles_max,),
            in_specs=[pl.BlockSpec(memory_space=pl.ANY),
                      pl.BlockSpec(memory_space=pl.ANY),
                      pl.BlockSpec((None, None, d, hid), lambda i, te, nt: (layer, te[i], 0, 0)),
                      pl.BlockSpec((None, None, d, hid), lambda i, te, nt: (layer, te[i], 0, 0)),
                      pl.BlockSpec((None, None, hid, d), lambda i, te, nt: (layer, te[i], 0, 0))],
            out_specs=pl.BlockSpec((tm * pitch, HEAD_DIM), lambda i, te, nt: (i, 0)),
            scratch_shapes=[pltpu.SMEM((2 * tm,), jnp.int32),
                            pltpu.VMEM((2, tm * pitch, HEAD_DIM), F32),
                            pltpu.SemaphoreType.DMA((2,)),
                            pltpu.SemaphoreType.DMA((2,))],
        ),
        out_shape=jax.ShapeDtypeStruct((n_tiles_max * tm * pitch, HEAD_DIM), F32),
        compiler_params=_cparams("arbitrary"),
        name="moe_experts",
    )(tile_expert, n_tiles, src_tok, h3, w_gate, w_up, w_down)


def _ffn_kernel(x_ref, wg_ref, wu_ref, wd_ref, o_ref):
    x = x_ref[...]
    a = _dot(x, wg_ref[...])
    u = _dot(x, wu_ref[...])
    hmid = (a * jax.nn.sigmoid(a) * u).astype(BF16)
    o_ref[...] = _dot(hmid, wd_ref[...]).astype(o_ref.dtype)


def _shared_ffn(h, w_gate, w_up, w_down, layer, tm=512):
    t, d = h.shape
    hid = w_gate.shape[-1]
    tm = _tile(t, tm)
    return pl.pallas_call(
        _ffn_kernel,
        grid=(t // tm,),
        in_specs=[pl.BlockSpec((tm, d), lambda i: (i, 0)),
                  pl.BlockSpec((None, d, hid), lambda i: (layer, 0, 0)),
                  pl.BlockSpec((None, d, hid), lambda i: (layer, 0, 0)),
                  pl.BlockSpec((None, hid, d), lambda i: (layer, 0, 0))],
        out_specs=pl.BlockSpec((tm, d), lambda i: (i, 0)),
        out_shape=jax.ShapeDtypeStruct((t, d), BF16),
        compiler_params=_cparams("arbitrary"),
        name="moe_shared",
    )(h, w_gate, w_up, w_down)


def _combine_ln_kernel(*refs, alpha, ts, nc, pitch, n_steps, with_h):
    it = iter(refs)
    idx_hbm, ys_hbm, x_ref, sh_ref, wk_ref, gate_ref, g_ref, b_ref = [next(it) for _ in range(8)]
    if with_h:
        shift_ref, scale_ref = next(it), next(it)
    xo_ref = next(it)
    if with_h:
        h_ref = next(it)
    idx_smem, buf, idx_sem, g_sem = [next(it) for _ in range(4)]

    i = pl.program_id(0)
    _gather_pipeline(i, n_steps, idx_hbm, ys_hbm, idx_smem, buf, idx_sem, g_sem,
                     n_idx=TOP_K * ts, nc=nc, pitch=pitch)
    slabs = buf.at[i % 2]
    wk = wk_ref[...]
    wkb = [jnp.broadcast_to(wk[:, k:k + 1], (ts, HEAD_DIM)) for k in range(TOP_K)]
    pieces = []
    for c in range(nc):
        acc = None
        for k in range(TOP_K):
            term = wkb[k] * slabs[pl.ds(k * ts * pitch + c, ts, stride=pitch), :]
            acc = term if acc is None else acc + term
        pieces.append(acc)
    y = sh_ref[...].astype(F32) + jnp.concatenate(pieces, axis=1)
    z = alpha * x_ref[...] + gate_ref[...] * y
    zc = z - jnp.mean(z, axis=-1, keepdims=True)
    var = jnp.mean(zc * zc, axis=-1, keepdims=True)
    xn = zc * lax.rsqrt(var + LN_EPS) * g_ref[...] + b_ref[...]
    xo_ref[...] = xn
    if with_h:
        h_ref[...] = (xn * (1.0 + scale_ref[...]) + shift_ref[...]).astype(h_ref.dtype)


def _combine_ln(x, shared, ys, dest, wk, gate, g, b, alpha, nc, pitch, shift=None, scale=None, ts=64):
    bsz, s, d = x.shape
    t = bsz * s
    ts = _tile(s, ts)
    nsb = s // ts
    n_steps = t // ts
    with_h = shift is not None
    idx = (dest * pitch).reshape(TOP_K, n_steps, ts).transpose(1, 0, 2).reshape(-1)
    row = pl.BlockSpec((ts, d), lambda i: (i, 0))
    vec = pl.BlockSpec((None, 1, d), lambda i: (i // nsb, 0, 0))
    par = pl.BlockSpec((1, d), lambda i: (0, 0))
    hbm = pl.BlockSpec(memory_space=pl.ANY)
    args = [idx, ys, x.reshape(t, d), shared, wk, gate, g.reshape(1, d), b.reshape(1, d)]
    specs = [hbm, hbm, row, row, pl.BlockSpec((ts, TOP_K), lambda i: (i, 0)), vec, par, par]
    if with_h:
        args += [shift, scale]
        specs += [vec, vec]
    out_shape = [jax.ShapeDtypeStruct((t, d), F32)]
    out_specs = [row]
    if with_h:
        out_shape.append(jax.ShapeDtypeStruct((t, d), BF16))
        out_specs.append(row)
    outs = pl.pallas_call(
        functools.partial(_combine_ln_kernel, alpha=alpha, ts=ts, nc=nc, pitch=pitch, n_steps=n_steps, with_h=with_h),
        grid=(n_steps,),
        in_specs=specs,
        out_specs=out_specs,
        out_shape=out_shape,
        scratch_shapes=[pltpu.SMEM((2 * TOP_K * ts,), jnp.int32),
                        pltpu.VMEM((2, TOP_K * ts * pitch, HEAD_DIM), F32),
                        pltpu.SemaphoreType.DMA((2,)),
                        pltpu.SemaphoreType.DMA((2,))],
        compiler_params=_cparams("arbitrary"),
        name="moe_combine_ln",
    )(*args)
    x_new = outs[0].reshape(bsz, s, d)
    return (x_new, outs[1].reshape(bsz, s, d)) if with_h else (x_new, None)


def _moe(h, h3, router_w, router_bias, w_gate, w_up, w_down, s_gate, s_up, s_down, layer, nc, pitch, tm=256):
    t, d = h.shape
    n_e = router_w.shape[1]
    tm = min(tm, t)
    top_e, top_w, rank, counts = _router(h, router_w.T.astype(F32), router_bias)

    counts = counts.astype(jnp.int32)
    padded = ((counts + tm - 1) // tm) * tm
    ends = jnp.cumsum(padded)
    offsets = ends - padded
    n_tiles_max = (t * TOP_K + n_e * (tm - 1)) // tm
    n_tiles = (ends[-1] // tm).astype(jnp.int32).reshape(1)
    tile_start = jnp.arange(n_tiles_max, dtype=jnp.int32) * tm
    tile_expert = jnp.minimum(jnp.sum((ends[None, :] <= tile_start[:, None]).astype(jnp.int32), axis=1), n_e - 1)
    onehot = top_e[:, :, None] == jnp.arange(n_e, dtype=jnp.int32)[None, None, :]
    dest = rank + jnp.sum(jnp.where(onehot, offsets[None, None, :], 0), axis=-1)
    tok_row0 = jnp.broadcast_to(jnp.arange(t, dtype=jnp.int32)[None, :] * pitch, dest.shape)
    src_row0 = jnp.zeros((n_tiles_max * tm,), jnp.int32).at[dest.reshape(-1)].set(
        tok_row0.reshape(-1), unique_indices=True)

    ys = _routed_experts(h3, src_row0, tile_expert, n_tiles, w_gate, w_up, w_down, layer, tm, nc, pitch)
    shared = _shared_ffn(h, s_gate, s_up, s_down, layer)
    return shared, ys, dest, top_w.T


def kernel(x, c, w_ada, b_ada, ada_table, ln_gain, ln_bias, gqa_wqkv, gqa_wo, gqa_q_gain, gqa_k_gain, dil_wqkv, dil_wo, nat_wqkv, nat_wo, nat_rpb, dif_wqkv, dif_wo, dif_lambda, dif_subln_gain, router_w, router_bias, exp_w_gate, exp_w_up, exp_w_down, sh_w_gate, sh_w_up, sh_w_down):
    bsz, seq, d = x.shape
    depth = ada_table.shape[0]
    t = bsz * seq
    alpha = (2 * depth) ** 0.25
    n_h = d // HEAD_DIM
    n_kv = n_h // GQA_RATIO
    n_h2 = d // (2 * HEAD_DIM)
    n_kv2 = n_h2 // GQA_RATIO
    n_grp = len(DIL_PATTERNS)
    nc = d // HEAD_DIM
    pitch = _slab_pitch(nc)

    pos = jnp.arange(seq)
    cos1, sin1 = _rope_tables(pos, HEAD_DIM)
    cos1_t = jnp.concatenate([cos1, cos1], axis=-1)
    sin1_t = jnp.concatenate([-sin1, sin1], axis=-1)
    cr, sr = _rope_tables(pos // GRID_W, HEAD_DIM // 2)
    cc, sc = _rope_tables(pos % GRID_W, HEAD_DIM // 2)
    cos2_t = jnp.concatenate([cr, cr, cc, cc], axis=-1)
    sin2_t = jnp.concatenate([-sr, sr, -sc, sc], axis=-1)

    pad_rows = 16
    c_pad = jnp.zeros((pad_rows, d), F32).at[:bsz].set(c.astype(F32))
    mod_shared = _ada_proj(c_pad, w_ada, b_ada)[:bsz].reshape(bsz, N_MOD, d)

    def mod_vec(mod, j):
        return mod[:, j:j + 1, :]

    expert_w = [w.astype(BF16) for w in (exp_w_gate, exp_w_up, exp_w_down)]
    shared_w = [w.astype(BF16) for w in (sh_w_gate, sh_w_up, sh_w_down)]

    mod = mod_shared + ada_table[0]
    h = _modulate(x, mod_vec(mod, 0), mod_vec(mod, 1))
    for i in range(depth):
        kind, j = i % N_MIXERS, i // N_MIXERS
        mod = mod_shared + ada_table[i]
        h2 = h.reshape(t, d)
        if kind == 0:
            qkv = _matmul(h2, gqa_wqkv[j], F32, name="gqa_qkv")
            flags = [3] * (n_h + n_kv) + [0] * n_kv
            gains = jnp.concatenate([jnp.broadcast_to(gqa_q_gain[j], (n_h, HEAD_DIM)),
                                     jnp.broadcast_to(gqa_k_gain[j], (n_kv, HEAD_DIM)),
                                     jnp.ones((n_kv, HEAD_DIM), F32)]).astype(F32)
            qkv = _qk_prep(qkv, flags, gains, cos2_t, sin2_t, HEAD_DIM // 4, seq)
            o = _gqa_attention(qkv, bsz, seq, n_kv)
            wo = gqa_wo[j]
        elif kind == 1:
            qkv = _matmul(h2, dil_wqkv[j], F32, name="dil_qkv")
            n_qk = n_grp * (n_h2 + n_kv2)
            flags = [2] * n_qk + [0] * (n_grp * n_kv2)
            gains = jnp.ones((n_qk + n_grp * n_kv2, HEAD_DIM), F32)
            qkv = _qk_prep(qkv, flags, gains, cos1_t, sin1_t, HEAD_DIM // 2, seq)
            outs, stats = [], []
            for grp, (window, dilation) in enumerate(DIL_PATTERNS):
                og, sg = _dilated_group(qkv, grp, n_grp, window, dilation, bsz, seq, n_kv2)
                outs.append(og)
                stats.append(sg)
            o = _dilated_merge(outs, stats, n_kv2)
            wo = dil_wo[j]
        elif kind == 2:
            qkv = _matmul(h2, nat_wqkv[j], BF16, name="nat_qkv")
            o = _nat_attention(qkv, _nat_bias_table(nat_rpb[j], n_kv), bsz, seq, n_kv)
            wo = nat_wo[j]
        else:
            qkv = _matmul(h2, dif_wqkv[j], F32, name="dif_qkv")
            n_qk = 2 * n_h2 + 2 * n_kv2
            flags = [2] * n_qk + [0] * (2 * n_kv2)
            gains = jnp.ones((n_qk + 2 * n_kv2, HEAD_DIM), F32)
            qkv = _qk_prep(qkv, flags, gains, cos1_t, sin1_t, HEAD_DIM // 2, seq)
            lam_init = 0.8 - 0.6 * math.exp(-0.3 * i)
            o = _diff_attention(qkv, dif_lambda[j], dif_subln_gain[j], lam_init, bsz, seq, n_kv2)
            wo = dif_wo[j]
        y = _matmul(o, wo, BF16, name="mixer_out").reshape(bsz, seq, d)
        x, h, h3 = _ln_residual(x, y, mod_vec(mod, 2), ln_gain[i, 0], ln_bias[i, 0], alpha,
                                mod_vec(mod, 3), mod_vec(mod, 4), nc, pitch)

        shared, ys, dest, wk = _moe(h.reshape(t, d), h3, router_w[i], router_bias[i],
                                    *expert_w, *shared_w, i, nc, pitch)
        if i + 1 < depth:
            nxt = mod_shared + ada_table[i + 1]
            shift_n, scale_n = mod_vec(nxt, 0), mod_vec(nxt, 1)
        else:
            shift_n = scale_n = None
        x, h = _combine_ln(x, shared, ys, dest, wk, mod_vec(mod, 5), ln_gain[i, 1], ln_bias[i, 1], alpha,
                           nc, pitch, shift=shift_n, scale=scale_n)
    return x
```

```python
import functools
import math

import jax
import jax.numpy as jnp
import numpy as np
from jax import lax
from jax.experimental import pallas as pl
from jax.experimental.pallas import tpu as pltpu

HEAD_DIM = 128
N_MIXERS = 4
GQA_RATIO = 4
GRID_W = 64
ROPE_THETA = 10000.0
LN_EPS = 1e-5
RMS_EPS = 1e-6
N_MOD = 6
DIL_PATTERNS = ((128, 1), (512, 4), (2048, 16))
NAT_WIN_ROWS = 8
NAT_WIN_COLS = 16
N_EXPERT_GROUPS = 8
TOPK_GROUPS = 4
TOP_K = 8
ROUTED_SCALE = 2.5

VMEM_LIMIT_BYTES = 56 * 1024 * 1024
NEG_BIG = -1e30

F32 = jnp.float32
BF16 = jnp.bfloat16


def _cparams(*sem):
    return pltpu.CompilerParams(dimension_semantics=("arbitrary",) * len(sem), vmem_limit_bytes=VMEM_LIMIT_BYTES)


def _tile(n, pref):
    if n <= pref:
        return n
    while n % pref:
        pref //= 2
    return pref


def _dot(a, b):
    return jnp.dot(a, b, preferred_element_type=F32)


def _dot_nt(a, b):
    return lax.dot_general(a, b, (((1,), (1,)), ((), ())), preferred_element_type=F32)


def _matmul_kernel(a_ref, w_ref, o_ref, acc_ref, *, nk):
    k = pl.program_id(2)

    @pl.when(k == 0)
    def _():
        acc_ref[...] = jnp.zeros_like(acc_ref)

    acc_ref[...] += _dot(a_ref[...].astype(BF16), w_ref[...].astype(BF16))

    @pl.when(k == nk - 1)
    def _():
        o_ref[...] = acc_ref[...].astype(o_ref.dtype)


def _matmul(a, w, out_dtype, tm=2048, tn=1024, tk=512, name="matmul"):
    m, kd = a.shape
    n = w.shape[1]
    tm, tn, tk = _tile(m, tm), _tile(n, tn), _tile(kd, tk)
    return pl.pallas_call(
        functools.partial(_matmul_kernel, nk=kd // tk),
        grid=(m // tm, n // tn, kd // tk),
        in_specs=[pl.BlockSpec((tm, tk), lambda i, j, k: (i, k)),
                  pl.BlockSpec((tk, tn), lambda i, j, k: (k, j))],
        out_specs=pl.BlockSpec((tm, tn), lambda i, j, k: (i, j)),
        out_shape=jax.ShapeDtypeStruct((m, n), out_dtype),
        scratch_shapes=[pltpu.VMEM((tm, tn), F32)],
        compiler_params=_cparams("parallel", "parallel", "arbitrary"),
        name=name,
    )(a, w)


def _ada_kernel(c_ref, w_ref, b_ref, o_ref, acc_ref, *, nk):
    k = pl.program_id(1)

    @pl.when(k == 0)
    def _():
        acc_ref[...] = jnp.zeros_like(acc_ref)

    c = c_ref[...]
    a = c * jax.nn.sigmoid(c)
    acc_ref[...] += _dot(a.astype(BF16), w_ref[...].astype(BF16))

    @pl.when(k == nk - 1)
    def _():
        o_ref[...] = acc_ref[...] + b_ref[...]


def _ada_proj(c_pad, w_ada, b_ada, tn=2048, tk=512):
    m, kd = c_pad.shape
    n = w_ada.shape[1]
    tn, tk = min(tn, n), min(tk, kd)
    return pl.pallas_call(
        functools.partial(_ada_kernel, nk=kd // tk),
        grid=(n // tn, kd // tk),
        in_specs=[pl.BlockSpec((m, tk), lambda j, k: (0, k)),
                  pl.BlockSpec((tk, tn), lambda j, k: (k, j)),
                  pl.BlockSpec((1, tn), lambda j, k: (0, j))],
        out_specs=pl.BlockSpec((m, tn), lambda j, k: (0, j)),
        out_shape=jax.ShapeDtypeStruct((m, n), F32),
        scratch_shapes=[pltpu.VMEM((m, tn), F32)],
        compiler_params=_cparams("parallel", "arbitrary"),
        name="ada_proj",
    )(c_pad, w_ada, b_ada.reshape(1, n))


def _modulate_kernel(x_ref, shift_ref, scale_ref, h_ref):
    h_ref[...] = (x_ref[...] * (1.0 + scale_ref[...]) + shift_ref[...]).astype(h_ref.dtype)


def _modulate(x, shift, scale, ts=512):
    b, s, d = x.shape
    ts = min(ts, s)
    row = pl.BlockSpec((None, ts, d), lambda bi, i: (bi, i, 0))
    vec = pl.BlockSpec((None, 1, d), lambda bi, i: (bi, 0, 0))
    return pl.pallas_call(
        _modulate_kernel,
        grid=(b, s // ts),
        in_specs=[row, vec, vec],
        out_specs=row,
        out_shape=jax.ShapeDtypeStruct((b, s, d), BF16),
        compiler_params=_cparams("parallel", "parallel"),
        name="modulate",
    )(x, shift, scale)


def _ln_kernel(x_ref, y_ref, gate_ref, g_ref, b_ref, shift_ref, scale_ref, xo_ref, h_ref, h3_ref, *, alpha, nc, pitch):
    ts = x_ref.shape[0]
    z = alpha * x_ref[...] + gate_ref[...] * y_ref[...].astype(F32)
    zc = z - jnp.mean(z, axis=-1, keepdims=True)
    var = jnp.mean(zc * zc, axis=-1, keepdims=True)
    xn = zc * lax.rsqrt(var + LN_EPS) * g_ref[...] + b_ref[...]
    xo_ref[...] = xn
    hf = xn * (1.0 + scale_ref[...]) + shift_ref[...]
    h_ref[...] = hf.astype(h_ref.dtype)
    _slab_store(h3_ref, hf, ts, nc, pitch)


def _ln_residual(x, y, gate, g, b, alpha, shift, scale, nc, pitch, ts=128):
    bsz, s, d = x.shape
    ts = _tile(s, ts)
    nsb = s // ts
    row = pl.BlockSpec((None, ts, d), lambda bi, i: (bi, i, 0))
    vec = pl.BlockSpec((None, 1, d), lambda bi, i: (bi, 0, 0))
    par = pl.BlockSpec((1, d), lambda bi, i: (0, 0))
    return pl.pallas_call(
        functools.partial(_ln_kernel, alpha=alpha, nc=nc, pitch=pitch),
        grid=(bsz, nsb),
        in_specs=[row, row, vec, par, par, vec, vec],
        out_specs=[row, row, pl.BlockSpec((ts * pitch, HEAD_DIM), lambda bi, i: (bi * nsb + i, 0))],
        out_shape=[jax.ShapeDtypeStruct((bsz, s, d), F32),
                   jax.ShapeDtypeStruct((bsz, s, d), BF16),
                   jax.ShapeDtypeStruct((bsz * s * pitch, HEAD_DIM), U32)],
        compiler_params=_cparams("arbitrary", "arbitrary"),
        name="ln_mixer",
    )(x, y, gate, g.reshape(1, d), b.reshape(1, d), shift, scale)


def _prep_kernel(x_ref, gain_ref, cos_ref, sin_ref, *o_refs, flags, half, dests):
    cos = cos_ref[...]
    sin = sin_ref[...]
    lane = lax.broadcasted_iota(jnp.int32, cos.shape, 1)
    first_half = (lane % (2 * half)) < half
    for j, flag in enumerate(flags):
        sl = slice(j * HEAD_DIM, (j + 1) * HEAD_DIM)
        o_ref = o_refs[dests[j][0]]
        osl = slice(dests[j][1] * HEAD_DIM, (dests[j][1] + 1) * HEAD_DIM)
        x = x_ref[:, sl].astype(F32)
        if flag & 1:
            x = x * lax.rsqrt(jnp.mean(x * x, axis=-1, keepdims=True) + RMS_EPS) * gain_ref[j:j + 1, :]
        if flag & 2:
            if 2 * half == HEAD_DIM:
                partner = pltpu.roll(x, half, 1)
            else:
                up = pltpu.roll(x, HEAD_DIM - half, 1)
                dn = pltpu.roll(x, half, 1)
                partner = jnp.where(first_half, up, dn)
            x = x * cos + partner * sin
        o_ref[:, osl] = x.astype(o_ref.dtype)


def _qk_prep(qkv, flags, gains, cos_t, sin_t, half, seq, groups=None, ts=256):
    t, n = qkv.shape
    nc = n // HEAD_DIM
    ts = _tile(seq, ts)
    nsb = seq // ts
    groups = [list(range(nc))] if groups is None else groups
    dests = {j: (gi, jj) for gi, grp in enumerate(groups) for jj, j in enumerate(grp)}
    outs = pl.pallas_call(
        functools.partial(_prep_kernel, flags=tuple(flags), half=half, dests=dests),
        grid=(t // ts,),
        in_specs=[pl.BlockSpec((ts, n), lambda i: (i, 0)),
                  pl.BlockSpec((nc, HEAD_DIM), lambda i: (0, 0)),
                  pl.BlockSpec((ts, HEAD_DIM), lambda i: (i % nsb, 0)),
                  pl.BlockSpec((ts, HEAD_DIM), lambda i: (i % nsb, 0))],
        out_specs=[pl.BlockSpec((ts, len(grp) * HEAD_DIM), lambda i: (i, 0)) for grp in groups],
        out_shape=[jax.ShapeDtypeStruct((t, len(grp) * HEAD_DIM), BF16) for grp in groups],
        compiler_params=_cparams("arbitrary"),
        name="qk_prep",
    )(qkv, gains, cos_t, sin_t)
    return outs[0] if len(groups) == 1 else outs


def _rope_tables(pos, dim):
    inv = ROPE_THETA ** (-jnp.arange(0, dim, 2, dtype=F32) / dim)
    ang = pos.astype(F32)[:, None] * inv[None, :]
    return jnp.cos(ang), jnp.sin(ang)


def _softmax_pv(s, v):
    m = jnp.max(s, axis=-1, keepdims=True)
    p = jnp.exp(s - m)
    l = jnp.sum(p, axis=-1, keepdims=True)
    return _dot(p.astype(BF16), v) / l


def _dense_head(q, k, v, scale):
    return _softmax_pv(_dot_nt(q, k) * scale, v)


def _gqa_attn_kernel(q_ref, k_ref, v_ref, o_ref, *, n_q, scale):
    k = k_ref[...]
    v = v_ref[...]
    for j in range(n_q):
        sl = slice(j * HEAD_DIM, (j + 1) * HEAD_DIM)
        o_ref[:, sl] = _dense_head(q_ref[:, sl], k, v, scale).astype(o_ref.dtype)


def _diff_attn_kernel(q_ref, k_ref, v_ref, lam_ref, gain_ref, o_ref, *, n_g, scale, lam_init):
    lp = lam_ref[...]
    lam = (jnp.exp(jnp.sum(lp[0:1] * lp[1:2], axis=-1, keepdims=True))
           - jnp.exp(jnp.sum(lp[2:3] * lp[3:4], axis=-1, keepdims=True)) + lam_init)
    v = v_ref[...]
    dv = v.shape[-1]
    for g in range(n_g):
        o = None
        for c in range(2):
            j = 2 * g + c
            q = q_ref[:, j * HEAD_DIM:(j + 1) * HEAD_DIM]
            oc = _dense_head(q, k_ref[:, c * HEAD_DIM:(c + 1) * HEAD_DIM], v, scale)
            o = oc if c == 0 else o - lam * oc
        o = o * lax.rsqrt(jnp.mean(o * o, axis=-1, keepdims=True) + RMS_EPS) * gain_ref[...]
        o_ref[:, g * dv:(g + 1) * dv] = (o * (1.0 - lam_init)).astype(o_ref.dtype)


def _gqa_attention(qkv, bsz, seq, n_kv, tq=256):
    g = GQA_RATIO
    n_h = n_kv * g
    tq = min(tq, seq)
    nq = seq // tq
    qw = g * HEAD_DIM
    return pl.pallas_call(
        functools.partial(_gqa_attn_kernel, n_q=g, scale=HEAD_DIM ** -0.5),
        grid=(bsz, n_kv, nq),
        in_specs=[pl.BlockSpec((tq, qw), lambda b, h, i: (b * nq + i, h)),
                  pl.BlockSpec((seq, HEAD_DIM), lambda b, h, i: (b, n_h + h)),
                  pl.BlockSpec((seq, HEAD_DIM), lambda b, h, i: (b, n_h + n_kv + h))],
        out_specs=pl.BlockSpec((tq, qw), lambda b, h, i: (b * nq + i, h)),
        out_shape=jax.ShapeDtypeStruct((bsz * seq, n_h * HEAD_DIM), BF16),
        compiler_params=_cparams("parallel", "parallel", "arbitrary"),
        name="gqa_attention",
    )(qkv, qkv, qkv)


def _diff_attention(qkv, lam_params, sub_gain, lam_init, bsz, seq, n_kv, tq=256):
    g = GQA_RATIO
    n_h = n_kv * g
    tq = min(tq, seq)
    nq = seq // tq
    qw = g * 2 * HEAD_DIM
    dv = 2 * HEAD_DIM
    k_blk0 = (2 * n_h * HEAD_DIM) // dv
    v_blk0 = k_blk0 + n_kv
    return pl.pallas_call(
        functools.partial(_diff_attn_kernel, n_g=g, scale=HEAD_DIM ** -0.5, lam_init=lam_init),
        grid=(bsz, n_kv, nq),
        in_specs=[pl.BlockSpec((tq, qw), lambda b, h, i: (b * nq + i, h)),
                  pl.BlockSpec((seq, dv), lambda b, h, i: (b, k_blk0 + h)),
                  pl.BlockSpec((seq, dv), lambda b, h, i: (b, v_blk0 + h)),
                  pl.BlockSpec((4, HEAD_DIM), lambda b, h, i: (0, 0)),
                  pl.BlockSpec((1, dv), lambda b, h, i: (0, 0))],
        out_specs=pl.BlockSpec((tq, g * dv), lambda b, h, i: (b * nq + i, h)),
        out_shape=jax.ShapeDtypeStruct((bsz * seq, n_h * dv), BF16),
        compiler_params=_cparams("parallel", "parallel", "arbitrary"),
        name="diff_attention",
    )(qkv, qkv, qkv, lam_params.astype(F32), sub_gain.reshape(1, dv).astype(F32))


def _dil_attn_kernel(q_ref, k_ref, v_ref, o_ref, st_ref, *, n_g, half, sub, scale):
    tl = q_ref.shape[0]
    length = k_ref.shape[0]
    win = min(length, sub + 2 * half)
    i = pl.program_id(3)
    lane = lax.broadcasted_iota(jnp.int32, (sub, HEAD_DIM), 1)
    for sb in range(tl // sub):
        qs = i * tl + sb * sub
        start = jnp.clip(qs - half, 0, length - win)
        start = pl.multiple_of(start, half)
        kw = k_ref[pl.ds(start, win), :]
        vw = v_ref[pl.ds(start, win), :]
        kpos = start + lax.broadcasted_iota(jnp.int32, (sub, win), 1)
        qpos = qs + lax.broadcasted_iota(jnp.int32, (sub, win), 0)
        valid = jnp.abs(kpos - qpos) <= half
        stats = jnp.zeros((sub, HEAD_DIM), F32)
        for g in range(n_g):
            sl = slice(g * HEAD_DIM, (g + 1) * HEAD_DIM)
            s = _dot_nt(q_ref[sb * sub:(sb + 1) * sub, sl], kw) * scale
            s = jnp.where(valid, s, NEG_BIG)
            m = jnp.max(s, axis=-1, keepdims=True)
            p = jnp.exp(s - m)
            l = jnp.sum(p, axis=-1, keepdims=True)
            o_ref[sb * sub:(sb + 1) * sub, sl] = (_dot(p.astype(BF16), vw) / l).astype(o_ref.dtype)
            stats = jnp.where(lane == g, m, stats)
            stats = jnp.where(lane == n_g + g, l, stats)
        st_ref[sb * sub:(sb + 1) * sub, :] = stats


def _dilated_group(qkv, window, dilation, bsz, seq, n_kv, tl=512, sub=128):
    g = GQA_RATIO
    n_h = n_kv * g
    d = dilation
    length = seq // d
    half = window // (2 * d)
    sub = min(sub, length)
    tl = min(tl, length)
    nl = length // tl
    n_cols = qkv.shape[1]
    qkv_v = qkv.reshape(bsz * length, d * n_cols)
    qw = g * HEAD_DIM
    assert n_cols % qw == 0 and length % tl == 0 and tl % sub == 0
    q_blk = lambda b, r, h, i: (b * nl + i, (r * n_cols) // qw + h)
    k_col0 = n_h * HEAD_DIM
    v_col0 = (n_h + n_kv) * HEAD_DIM
    k_blk = lambda b, r, h, i: (b, (r * n_cols + k_col0) // HEAD_DIM + h)
    v_blk = lambda b, r, h, i: (b, (r * n_cols + v_col0) // HEAD_DIM + h)
    o, st = pl.pallas_call(
        functools.partial(_dil_attn_kernel, n_g=g, half=half, sub=sub, scale=HEAD_DIM ** -0.5),
        grid=(bsz, d, n_kv, nl),
        in_specs=[pl.BlockSpec((tl, qw), q_blk),
                  pl.BlockSpec((length, HEAD_DIM), k_blk),
                  pl.BlockSpec((length, HEAD_DIM), v_blk)],
        out_specs=[pl.BlockSpec((tl, qw), lambda b, r, h, i: (b * nl + i, r * n_kv + h)),
                   pl.BlockSpec((tl, HEAD_DIM), lambda b, r, h, i: (b * nl + i, r * n_kv + h))],
        out_shape=[jax.ShapeDtypeStruct((bsz * length, d * n_h * HEAD_DIM), F32),
                   jax.ShapeDtypeStruct((bsz * length, d * n_kv * HEAD_DIM), F32)],
        compiler_params=_cparams("parallel", "parallel", "parallel", "arbitrary"),
        name=f"dilated_attention_d{d}",
    )(qkv_v, qkv_v, qkv_v)
    return o.reshape(bsz * seq, n_h * HEAD_DIM), st.reshape(bsz * seq, n_kv * HEAD_DIM)


def _dil_merge_kernel(*refs, n_grp, n_kv, n_g):
    o_refs, st_refs, out_ref = refs[:n_grp], refs[n_grp:2 * n_grp], refs[2 * n_grp]
    for kv in range(n_kv):
        st = [r[:, kv * HEAD_DIM:(kv + 1) * HEAD_DIM] for r in st_refs]
        for g in range(n_g):
            h = kv * n_g + g
            ms = [s[:, g:g + 1] for s in st]
            ls = [s[:, n_g + g:n_g + g + 1] for s in st]
            mmax = functools.reduce(jnp.maximum, ms)
            ws = [l * jnp.exp(m - mmax) for m, l in zip(ms, ls)]
            wsum = functools.reduce(lambda a, b: a + b, ws)
            sl = slice(h * HEAD_DIM, (h + 1) * HEAD_DIM)
            acc = None
            for w, o_ref in zip(ws, o_refs):
                term = (w / wsum) * o_ref[:, sl]
                acc = term if acc is None else acc + term
            out_ref[:, sl] = acc.astype(out_ref.dtype)


def _dilated_merge(outs, stats, n_kv, ts=256):
    t, n = outs[0].shape
    ts = min(ts, t)
    n_grp = len(outs)
    o_spec = pl.BlockSpec((ts, n), lambda i: (i, 0))
    s_spec = pl.BlockSpec((ts, stats[0].shape[1]), lambda i: (i, 0))
    return pl.pallas_call(
        functools.partial(_dil_merge_kernel, n_grp=n_grp, n_kv=n_kv, n_g=GQA_RATIO),
        grid=(t // ts,),
        in_specs=[o_spec] * n_grp + [s_spec] * n_grp,
        out_specs=o_spec,
        out_shape=jax.ShapeDtypeStruct((t, n), BF16),
        compiler_params=_cparams("parallel"),
        name="dilated_merge",
    )(*outs, *stats)


def _nat_attn_kernel(q_ref, k_ref, v_ref, bias_ref, o_ref, *, n_g, rows, scale):
    rows_per_step = q_ref.shape[0] // GRID_W
    i = pl.program_id(2)
    nk = NAT_WIN_ROWS * GRID_W
    for rr in range(rows_per_step):
        r = i * rows_per_step + rr
        rs = jnp.clip(r - NAT_WIN_ROWS // 2, 0, rows - NAT_WIN_ROWS)
        ro = r - rs
        start = pl.multiple_of(rs * GRID_W, GRID_W)
        kw = k_ref[pl.ds(start, nk), :]
        vw = v_ref[pl.ds(start, nk), :]
        rsl = slice(rr * GRID_W, (rr + 1) * GRID_W)
        q = jnp.concatenate([q_ref[rsl, g * HEAD_DIM:(g + 1) * HEAD_DIM] for g in range(n_g)], axis=0)
        s = _dot_nt(q, kw) * scale + bias_ref[ro]
        o = _softmax_pv(s, vw)
        for g in range(n_g):
            o_ref[rsl, g * HEAD_DIM:(g + 1) * HEAD_DIM] = o[g * GRID_W:(g + 1) * GRID_W].astype(o_ref.dtype)


def _nat_bias_table(rpb, n_kv):
    n_h = rpb.shape[0]
    g = n_h // n_kv
    nr, ncol, w = NAT_WIN_ROWS, NAT_WIN_COLS, GRID_W
    ro, ki = np.meshgrid(np.arange(nr), np.arange(nr), indexing="ij")
    rowsel = np.zeros((nr, nr, 2 * nr - 1), np.float32)
    rowsel[ro, ki, ki - ro + nr - 1] = 1.0
    c, kc = np.meshgrid(np.arange(w), np.arange(w), indexing="ij")
    cs = np.clip(c - ncol // 2, 0, w - ncol)
    valid = (kc >= cs) & (kc < cs + ncol)
    colsel = np.zeros((2 * ncol - 1, w, w), np.float32)
    colsel[(kc - c + ncol - 1)[valid], c[valid], kc[valid]] = 1.0
    rows = jnp.einsum("hij,rki->hrkj", rpb.astype(F32), rowsel, precision=lax.Precision.HIGHEST)
    tab = jnp.einsum("hrkj,jcq->hrckq", rows, colsel, precision=lax.Precision.HIGHEST)
    tab = jnp.where(valid[None, None, :, None, :], tab, NEG_BIG)
    tab = tab.reshape(n_kv, g, nr, w, nr * w)
    return tab.transpose(0, 2, 1, 3, 4).reshape(n_kv, nr, g * w, nr * w)


def _nat_attention(qkv, bias_tab, bsz, seq, n_kv, rows_per_step=8):
    g = GQA_RATIO
    n_h = n_kv * g
    rows = seq // GRID_W
    assert rows >= NAT_WIN_ROWS
    rows_per_step = min(rows_per_step, rows)
    tq = rows_per_step * GRID_W
    nq = seq // tq
    qw = g * HEAD_DIM
    return pl.pallas_call(
        functools.partial(_nat_attn_kernel, n_g=g, rows=rows, scale=HEAD_DIM ** -0.5),
        grid=(bsz, n_kv, nq),
        in_specs=[pl.BlockSpec((tq, qw), lambda b, h, i: (b * nq + i, h)),
                  pl.BlockSpec((seq, HEAD_DIM), lambda b, h, i: (b, n_h + h)),
                  pl.BlockSpec((seq, HEAD_DIM), lambda b, h, i: (b, n_h + n_kv + h)),
                  pl.BlockSpec((None,) + bias_tab.shape[1:], lambda b, h, i: (h, 0, 0, 0))],
        out_specs=pl.BlockSpec((tq, qw), lambda b, h, i: (b * nq + i, h)),
        out_shape=jax.ShapeDtypeStruct((bsz * seq, n_h * HEAD_DIM), BF16),
        compiler_params=_cparams("parallel", "parallel", "arbitrary"),
        name="nat_attention",
    )(qkv, qkv, qkv, bias_tab)


def _router_kernel(h_ref, w_ref, bias_ref, e_ref, wt_ref, rank_ref, cnt_ref, carry_ref, *, n_e):
    i = pl.program_id(0)
    tm = h_ref.shape[0]
    per_group = n_e // N_EXPERT_GROUPS

    @pl.when(i == 0)
    def _():
        carry_ref[...] = jnp.zeros_like(carry_ref)

    h = h_ref[...]
    w = w_ref[...]
    w_hi = w.astype(BF16)
    w_lo = (w - w_hi.astype(F32)).astype(BF16)
    logits = _dot_nt(w_hi, h) + _dot_nt(w_lo, h)
    scores = jax.nn.sigmoid(logits)
    biased = scores + bias_ref[...]

    b3 = biased.reshape(N_EXPERT_GROUPS, per_group, tm)
    idx3 = lax.broadcasted_iota(jnp.int32, b3.shape, 1)
    m1 = jnp.max(b3, axis=1, keepdims=True)
    i1 = jnp.min(jnp.where(b3 == m1, idx3, per_group), axis=1, keepdims=True)
    m2 = jnp.max(jnp.where(idx3 == i1, -jnp.inf, b3), axis=1, keepdims=True)
    gs = (m1 + m2).reshape(N_EXPERT_GROUPS, tm)

    gidx = lax.broadcasted_iota(jnp.int32, gs.shape, 0)
    gsel = jnp.zeros(gs.shape, jnp.bool_)
    for _ in range(TOPK_GROUPS):
        gm = jnp.max(gs, axis=0, keepdims=True)
        gi = jnp.min(jnp.where(gs == gm, gidx, N_EXPERT_GROUPS), axis=0, keepdims=True)
        hit = gidx == gi
        gsel = gsel | hit
        gs = jnp.where(hit, -jnp.inf, gs)
    emask = jnp.broadcast_to(gsel.reshape(N_EXPERT_GROUPS, 1, tm), b3.shape).reshape(n_e, tm)
    masked = jnp.where(emask, biased, -jnp.inf)

    eidx = lax.broadcasted_iota(jnp.int32, masked.shape, 0)
    sel = jnp.zeros(masked.shape, jnp.bool_)
    top_e, top_w, hits = [], [], []
    for _ in range(TOP_K):
        mx = jnp.max(masked, axis=0, keepdims=True)
        ei = jnp.min(jnp.where(masked == mx, eidx, n_e), axis=0, keepdims=True)
        hit = eidx == ei
        top_e.append(ei)
        top_w.append(jnp.sum(jnp.where(hit, scores, 0.0), axis=0, keepdims=True))
        hits.append(hit)
        sel = sel | hit
        masked = jnp.where(hit, -jnp.inf, masked)
    wsum = functools.reduce(lambda a, b: a + b, top_w)

    selb = jnp.where(sel, 1.0, 0.0).astype(BF16)
    tri = (lax.broadcasted_iota(jnp.int32, (tm, tm), 0) < lax.broadcasted_iota(jnp.int32, (tm, tm), 1))
    ranks = _dot(selb, jnp.where(tri, 1.0, 0.0).astype(BF16)) + carry_ref[:, 0:1]
    carry_ref[...] = carry_ref[...] + jnp.sum(selb.astype(F32), axis=1, keepdims=True)

    for k in range(TOP_K):
        e_ref[k:k + 1, :] = top_e[k]
        wt_ref[k:k + 1, :] = top_w[k] / wsum * ROUTED_SCALE
        rank_ref[k:k + 1, :] = jnp.sum(jnp.where(hits[k], ranks, 0.0), axis=0, keepdims=True).astype(jnp.int32)
    cnt_ref[...] = carry_ref[...]


def _router(x, router_w_t, router_bias, tm=512):
    t, d = x.shape
    n_e = router_w_t.shape[0]
    tm = min(tm, t)
    tok = pl.BlockSpec((TOP_K, tm), lambda i: (0, i))
    e, w, rank, cnt = pl.pallas_call(
        functools.partial(_router_kernel, n_e=n_e),
        grid=(t // tm,),
        in_specs=[pl.BlockSpec((tm, d), lambda i: (i, 0)),
                  pl.BlockSpec((n_e, d), lambda i: (0, 0)),
                  pl.BlockSpec((n_e, 1), lambda i: (0, 0))],
        out_specs=[tok, tok, tok, pl.BlockSpec((n_e, HEAD_DIM), lambda i: (0, 0))],
        out_shape=[jax.ShapeDtypeStruct((TOP_K, t), jnp.int32),
                   jax.ShapeDtypeStruct((TOP_K, t), F32),
                   jax.ShapeDtypeStruct((TOP_K, t), jnp.int32),
                   jax.ShapeDtypeStruct((n_e, HEAD_DIM), F32)],
        scratch_shapes=[pltpu.VMEM((n_e, HEAD_DIM), F32)],
        compiler_params=_cparams("arbitrary"),
        name="moe_router",
    )(x, router_w_t, router_bias.reshape(n_e, 1).astype(F32))
    return e, w, rank, cnt[:, 0]


U32 = jnp.uint32
HI_MASK = 0xFFFF0000


def _slab_pitch(nc):
    pitch = nc + 8
    return pitch if (pitch // 8) % 2 else pitch + 8


def _bf16_bits(x):
    return lax.bitcast_convert_type(x.astype(BF16).astype(F32), U32)


def _slab_words(ref, c, n_tok, pitch, row0=0):
    w = ref[pl.ds(row0 + c, n_tok, stride=pitch), :]
    lo = lax.bitcast_convert_type(w << 16, F32)
    hi = lax.bitcast_convert_type(w & U32(HI_MASK), F32)
    return lo, hi


def _slab_load(ref, n_tok, nc, pitch):
    halves = [_slab_words(ref, c, n_tok, pitch) for c in range(nc)]
    return jnp.concatenate([lo.astype(BF16) for lo, _ in halves] + [hi.astype(BF16) for _, hi in halves], axis=1)


def _slab_store(ref, val, n_tok, nc, pitch):
    half = nc * HEAD_DIM
    for c in range(nc):
        lo = _bf16_bits(val[:, c * HEAD_DIM:(c + 1) * HEAD_DIM]) >> 16
        hi = _bf16_bits(val[:, half + c * HEAD_DIM:half + (c + 1) * HEAD_DIM]) & U32(HI_MASK)
        ref[pl.ds(c, n_tok, stride=pitch), :] = lo | hi
    for r in range(nc, pitch):
        ref[pl.ds(r, n_tok, stride=pitch), :] = jnp.zeros((n_tok, HEAD_DIM), U32)


def _gather_pipeline(i, n_act, idx_hbm, src_hbm, idx_smem, buf, idx_sem, g_sem, *, n_idx, nc, pitch):
    slot = i % 2
    rows = n_idx * nc

    def idx_copy(tile, sl):
        return pltpu.make_async_copy(idx_hbm.at[pl.ds(tile * n_idx, n_idx)],
                                     idx_smem.at[pl.ds(sl * n_idx, n_idx)], idx_sem.at[sl])

    def issue(sl):
        def body(j, carry):
            row0 = idx_smem[sl * n_idx + j]
            pltpu.make_async_copy(src_hbm.at[pl.ds(pl.multiple_of(row0, 8), nc), :],
                                  buf.at[sl, pl.ds(pl.multiple_of(j * pitch, 8), nc), :], g_sem.at[sl]).start()
            return carry
        lax.fori_loop(0, n_idx, body, 0, unroll=8)

    @pl.when(i == 0)
    def _():
        idx_copy(0, 0).start()
        idx_copy(0, 0).wait()
        issue(0)

        @pl.when(1 < n_act)
        def _():
            idx_copy(1, 1).start()

    @pl.when(i + 1 < n_act)
    def _():
        idx_copy(i + 1, 1 - slot).wait()
        issue(1 - slot)

    @pl.when(i + 2 < n_act)
    def _():
        idx_copy(i + 2, slot).start()

    pltpu.make_async_copy(src_hbm.at[pl.ds(0, rows), :], buf.at[slot, pl.ds(0, rows), :], g_sem.at[slot]).wait()


def _experts_kernel(te_ref, nt_ref, idx_hbm, h3_hbm, wg_ref, wu_ref, wd_ref, ys_ref,
                    idx_smem, xbuf, idx_sem, g_sem, *, tm, nc, pitch):
    i = pl.program_id(0)
    nt = nt_ref[0]

    @pl.when(i < nt)
    def _():
        _gather_pipeline(i, nt, idx_hbm, h3_hbm, idx_smem, xbuf, idx_sem, g_sem, n_idx=tm, nc=nc, pitch=pitch)
        x = _slab_load(xbuf.at[i % 2], tm, nc, pitch)
        a = _dot(x, wg_ref[...])
        u = _dot(x, wu_ref[...])
        hmid = (a * jax.nn.sigmoid(a) * u).astype(BF16)
        y = _dot(hmid, wd_ref[...])
        _slab_store(ys_ref, y, tm, nc, pitch)

    @pl.when(i >= nt)
    def _():
        ys_ref[...] = jnp.zeros_like(ys_ref)


def _routed_experts(h3, src_tok, tile_expert, n_tiles, w_gate, w_up, w_down, layer, tm, nc, pitch):
    n_tiles_max = tile_expert.shape[0]
    d = w_gate.shape[-2]
    hid = w_gate.shape[-1]
    return pl.pallas_call(
        functools.partial(_experts_kernel, tm=tm, nc=nc, pitch=pitch),
        grid_spec=pltpu.PrefetchScalarGridSpec(
            num_scalar_prefetch=2,
            grid=(n_tiles_max,),
            in_specs=[pl.BlockSpec(memory_space=pl.ANY),
                      pl.BlockSpec(memory_space=pl.ANY),
                      pl.BlockSpec((None, None, d, hid), lambda i, te, nt: (layer, te[i], 0, 0)),
                      pl.BlockSpec((None, None, d, hid), lambda i, te, nt: (layer, te[i], 0, 0)),
                      pl.BlockSpec((None, None, hid, d), lambda i, te, nt: (layer, te[i], 0, 0))],
            out_specs=pl.BlockSpec((tm * pitch, HEAD_DIM), lambda i, te, nt: (i, 0)),
            scratch_shapes=[pltpu.SMEM((2 * tm,), jnp.int32),
                            pltpu.VMEM((2, tm * pitch, HEAD_DIM), U32),
                            pltpu.SemaphoreType.DMA((2,)),
                            pltpu.SemaphoreType.DMA((2,))],
        ),
        out_shape=jax.ShapeDtypeStruct((n_tiles_max * tm * pitch, HEAD_DIM), U32),
        compiler_params=_cparams("arbitrary"),
        name="moe_experts",
    )(tile_expert, n_tiles, src_tok, h3, w_gate, w_up, w_down)


def _ffn_kernel(x_ref, wg_ref, wu_ref, wd_ref, o_ref):
    x = x_ref[...]
    a = _dot(x, wg_ref[...])
    u = _dot(x, wu_ref[...])
    hmid = (a * jax.nn.sigmoid(a) * u).astype(BF16)
    o_ref[...] = _dot(hmid, wd_ref[...]).astype(o_ref.dtype)


def _shared_ffn(h, w_gate, w_up, w_down, layer, tm=512):
    t, d = h.shape
    hid = w_gate.shape[-1]
    tm = _tile(t, tm)
    return pl.pallas_call(
        _ffn_kernel,
        grid=(t // tm,),
        in_specs=[pl.BlockSpec((tm, d), lambda i: (i, 0)),
                  pl.BlockSpec((None, d, hid), lambda i: (layer, 0, 0)),
                  pl.BlockSpec((None, d, hid), lambda i: (layer, 0, 0)),
                  pl.BlockSpec((None, hid, d), lambda i: (layer, 0, 0))],
        out_specs=pl.BlockSpec((tm, d), lambda i: (i, 0)),
        out_shape=jax.ShapeDtypeStruct((t, d), BF16),
        compiler_params=_cparams("arbitrary"),
        name="moe_shared",
    )(h, w_gate, w_up, w_down)


def _combine_ln_kernel(*refs, alpha, ts, nc, pitch, n_steps, with_h):
    it = iter(refs)
    idx_hbm, ys_hbm, x_ref, sh_ref, wk_ref, gate_ref, g_ref, b_ref = [next(it) for _ in range(8)]
    if with_h:
        shift_ref, scale_ref = next(it), next(it)
    xo_ref = next(it)
    if with_h:
        h_ref = next(it)
    idx_smem, buf, idx_sem, g_sem = [next(it) for _ in range(4)]

    i = pl.program_id(0)
    _gather_pipeline(i, n_steps, idx_hbm, ys_hbm, idx_smem, buf, idx_sem, g_sem,
                     n_idx=TOP_K * ts, nc=nc, pitch=pitch)
    slabs = buf.at[i % 2]
    wk = wk_ref[...]
    wkb = [jnp.broadcast_to(wk[:, k:k + 1], (ts, HEAD_DIM)) for k in range(TOP_K)]
    lo_pieces, hi_pieces = [], []
    for c in range(nc):
        acc_lo = acc_hi = None
        for k in range(TOP_K):
            lo, hi = _slab_words(slabs, c, ts, pitch, row0=k * ts * pitch)
            acc_lo = wkb[k] * lo if acc_lo is None else acc_lo + wkb[k] * lo
            acc_hi = wkb[k] * hi if acc_hi is None else acc_hi + wkb[k] * hi
        lo_pieces.append(acc_lo)
        hi_pieces.append(acc_hi)
    y = sh_ref[...].astype(F32) + jnp.concatenate(lo_pieces + hi_pieces, axis=1)
    z = alpha * x_ref[...] + gate_ref[...] * y
    zc = z - jnp.mean(z, axis=-1, keepdims=True)
    var = jnp.mean(zc * zc, axis=-1, keepdims=True)
    xn = zc * lax.rsqrt(var + LN_EPS) * g_ref[...] + b_ref[...]
    xo_ref[...] = xn
    if with_h:
        h_ref[...] = (xn * (1.0 + scale_ref[...]) + shift_ref[...]).astype(h_ref.dtype)


def _combine_ln(x, shared, ys, dest, wk, gate, g, b, alpha, nc, pitch, shift=None, scale=None, ts=128):
    bsz, s, d = x.shape
    t = bsz * s
    ts = _tile(s, ts)
    nsb = s // ts
    n_steps = t // ts
    with_h = shift is not None
    idx = (dest * pitch).reshape(TOP_K, n_steps, ts).transpose(1, 0, 2).reshape(-1)
    row = pl.BlockSpec((ts, d), lambda i: (i, 0))
    vec = pl.BlockSpec((None, 1, d), lambda i: (i // nsb, 0, 0))
    par = pl.BlockSpec((1, d), lambda i: (0, 0))
    hbm = pl.BlockSpec(memory_space=pl.ANY)
    args = [idx, ys, x.reshape(t, d), shared, wk, gate, g.reshape(1, d), b.reshape(1, d)]
    specs = [hbm, hbm, row, row, pl.BlockSpec((ts, TOP_K), lambda i: (i, 0)), vec, par, par]
    if with_h:
        args += [shift, scale]
        specs += [vec, vec]
    out_shape = [jax.ShapeDtypeStruct((t, d), F32)]
    out_specs = [row]
    if with_h:
        out_shape.append(jax.ShapeDtypeStruct((t, d), BF16))
        out_specs.append(row)
    outs = pl.pallas_call(
        functools.partial(_combine_ln_kernel, alpha=alpha, ts=ts, nc=nc, pitch=pitch, n_steps=n_steps, with_h=with_h),
        grid=(n_steps,),
        in_specs=specs,
        out_specs=out_specs,
        out_shape=out_shape,
        scratch_shapes=[pltpu.SMEM((2 * TOP_K * ts,), jnp.int32),
                        pltpu.VMEM((2, TOP_K * ts * pitch, HEAD_DIM), U32),
                        pltpu.SemaphoreType.DMA((2,)),
                        pltpu.SemaphoreType.DMA((2,))],
        compiler_params=_cparams("arbitrary"),
        name="moe_combine_ln",
    )(*args)
    x_new = outs[0].reshape(bsz, s, d)
    return (x_new, outs[1].reshape(bsz, s, d)) if with_h else (x_new, None)


def _moe(h, h3, router_w, router_bias, w_gate, w_up, w_down, s_gate, s_up, s_down, layer, nc, pitch, tm=256):
    t, d = h.shape
    n_e = router_w.shape[1]
    tm = min(tm, t)
    top_e, top_w, rank, counts = _router(h, router_w.T.astype(F32), router_bias)

    counts = counts.astype(jnp.int32)
    padded = ((counts + tm - 1) // tm) * tm
    ends = jnp.cumsum(padded)
    offsets = ends - padded
    n_tiles_max = (t * TOP_K + n_e * (tm - 1)) // tm
    n_tiles = (ends[-1] // tm).astype(jnp.int32).reshape(1)
    tile_start = jnp.arange(n_tiles_max, dtype=jnp.int32) * tm
    tile_expert = jnp.minimum(jnp.sum((ends[None, :] <= tile_start[:, None]).astype(jnp.int32), axis=1), n_e - 1)
    onehot = top_e[:, :, None] == jnp.arange(n_e, dtype=jnp.int32)[None, None, :]
    dest = rank + jnp.sum(jnp.where(onehot, offsets[None, None, :], 0), axis=-1)
    tok_row0 = jnp.broadcast_to(jnp.arange(t, dtype=jnp.int32)[None, :] * pitch, dest.shape)
    src_row0 = jnp.zeros((n_tiles_max * tm,), jnp.int32).at[dest.reshape(-1)].set(
        tok_row0.reshape(-1), unique_indices=True)

    ys = _routed_experts(h3, src_row0, tile_expert, n_tiles, w_gate, w_up, w_down, layer, tm, nc, pitch)
    shared = _shared_ffn(h, s_gate, s_up, s_down, layer)
    return shared, ys, dest, top_w.T


def kernel(x, c, w_ada, b_ada, ada_table, ln_gain, ln_bias, gqa_wqkv, gqa_wo, gqa_q_gain, gqa_k_gain, dil_wqkv, dil_wo, nat_wqkv, nat_wo, nat_rpb, dif_wqkv, dif_wo, dif_lambda, dif_subln_gain, router_w, router_bias, exp_w_gate, exp_w_up, exp_w_down, sh_w_gate, sh_w_up, sh_w_down):
    bsz, seq, d = x.shape
    depth = ada_table.shape[0]
    t = bsz * seq
    alpha = (2 * depth) ** 0.25
    n_h = d // HEAD_DIM
    n_kv = n_h // GQA_RATIO
    n_h2 = d // (2 * HEAD_DIM)
    n_kv2 = n_h2 // GQA_RATIO
    n_grp = len(DIL_PATTERNS)
    nc = d // (2 * HEAD_DIM)
    pitch = _slab_pitch(nc)

    pos = jnp.arange(seq)
    cos1, sin1 = _rope_tables(pos, HEAD_DIM)
    cos1_t = jnp.concatenate([cos1, cos1], axis=-1)
    sin1_t = jnp.concatenate([-sin1, sin1], axis=-1)
    cr, sr = _rope_tables(pos // GRID_W, HEAD_DIM // 2)
    cc, sc = _rope_tables(pos % GRID_W, HEAD_DIM // 2)
    cos2_t = jnp.concatenate([cr, cr, cc, cc], axis=-1)
    sin2_t = jnp.concatenate([-sr, sr, -sc, sc], axis=-1)

    pad_rows = 16
    c_pad = jnp.zeros((pad_rows, d), F32).at[:bsz].set(c.astype(F32))
    mod_shared = _ada_proj(c_pad, w_ada, b_ada)[:bsz].reshape(bsz, N_MOD, d)

    def mod_vec(mod, j):
        return mod[:, j:j + 1, :]

    expert_w = [w.astype(BF16) for w in (exp_w_gate, exp_w_up, exp_w_down)]
    shared_w = [w.astype(BF16) for w in (sh_w_gate, sh_w_up, sh_w_down)]

    mod = mod_shared + ada_table[0]
    h = _modulate(x, mod_vec(mod, 0), mod_vec(mod, 1))
    for i in range(depth):
        kind, j = i % N_MIXERS, i // N_MIXERS
        mod = mod_shared + ada_table[i]
        h2 = h.reshape(t, d)
        if kind == 0:
            qkv = _matmul(h2, gqa_wqkv[j], F32, name="gqa_qkv")
            flags = [3] * (n_h + n_kv) + [0] * n_kv
            gains = jnp.concatenate([jnp.broadcast_to(gqa_q_gain[j], (n_h, HEAD_DIM)),
                                     jnp.broadcast_to(gqa_k_gain[j], (n_kv, HEAD_DIM)),
                                     jnp.ones((n_kv, HEAD_DIM), F32)]).astype(F32)
            qkv = _qk_prep(qkv, flags, gains, cos2_t, sin2_t, HEAD_DIM // 4, seq)
            o = _gqa_attention(qkv, bsz, seq, n_kv)
            wo = gqa_wo[j]
        elif kind == 1:
            qkv = _matmul(h2, dil_wqkv[j], F32, name="dil_qkv")
            n_qk = n_grp * (n_h2 + n_kv2)
            flags = [2] * n_qk + [0] * (n_grp * n_kv2)
            gains = jnp.ones((n_qk + n_grp * n_kv2, HEAD_DIM), F32)
            k0, v0 = n_grp * n_h2, n_grp * (n_h2 + n_kv2)
            groups = [list(range(grp * n_h2, (grp + 1) * n_h2))
                      + list(range(k0 + grp * n_kv2, k0 + (grp + 1) * n_kv2))
                      + list(range(v0 + grp * n_kv2, v0 + (grp + 1) * n_kv2)) for grp in range(n_grp)]
            qkv_g = _qk_prep(qkv, flags, gains, cos1_t, sin1_t, HEAD_DIM // 2, seq, groups=groups)
            outs, stats = [], []
            for grp, (window, dilation) in enumerate(DIL_PATTERNS):
                og, sg = _dilated_group(qkv_g[grp], window, dilation, bsz, seq, n_kv2)
                outs.append(og)
                stats.append(sg)
            o = _dilated_merge(outs, stats, n_kv2)
            wo = dil_wo[j]
        elif kind == 2:
            qkv = _matmul(h2, nat_wqkv[j], BF16, name="nat_qkv")
            o = _nat_attention(qkv, _nat_bias_table(nat_rpb[j], n_kv), bsz, seq, n_kv)
            wo = nat_wo[j]
        else:
            qkv = _matmul(h2, dif_wqkv[j], F32, name="dif_qkv")
            n_qk = 2 * n_h2 + 2 * n_kv2
            flags = [2] * n_qk + [0] * (2 * n_kv2)
            gains = jnp.ones((n_qk + 2 * n_kv2, HEAD_DIM), F32)
            qkv = _qk_prep(qkv, flags, gains, cos1_t, sin1_t, HEAD_DIM // 2, seq)
            lam_init = 0.8 - 0.6 * math.exp(-0.3 * i)
            o = _diff_attention(qkv, dif_lambda[j], dif_subln_gain[j], lam_init, bsz, seq, n_kv2)
            wo = dif_wo[j]
        y = _matmul(o, wo, BF16, name="mixer_out").reshape(bsz, seq, d)
        x, h, h3 = _ln_residual(x, y, mod_vec(mod, 2), ln_gain[i, 0], ln_bias[i, 0], alpha,
                                mod_vec(mod, 3), mod_vec(mod, 4), nc, pitch)

        shared, ys, dest, wk = _moe(h.reshape(t, d), h3, router_w[i], router_bias[i],
                                    *expert_w, *shared_w, i, nc, pitch)
        if i + 1 < depth:
            nxt = mod_shared + ada_table[i + 1]
            shift_n, scale_n = mod_vec(nxt, 0), mod_vec(nxt, 1)
        else:
            shift_n = scale_n = None
        x, h = _combine_ln(x, shared, ys, dest, wk, mod_vec(mod, 5), ln_gain[i, 1], ln_bias[i, 1], alpha,
                           nc, pitch, shift=shift_n, scale=scale_n)
    return x
```

```python
import functools
import math

import jax
import jax.numpy as jnp
import numpy as np
from jax import lax
from jax.experimental import pallas as pl
from jax.experimental.pallas import tpu as pltpu

HEAD_DIM = 128
N_MIXERS = 4
GQA_RATIO = 4
GRID_W = 64
ROPE_THETA = 10000.0
LN_EPS = 1e-5
RMS_EPS = 1e-6
N_MOD = 6
DIL_PATTERNS = ((128, 1), (512, 4), (2048, 16))
NAT_WIN_ROWS = 8
NAT_WIN_COLS = 16
N_EXPERT_GROUPS = 8
TOPK_GROUPS = 4
TOP_K = 8
ROUTED_SCALE = 2.5

VMEM_LIMIT_BYTES = 56 * 1024 * 1024
NEG_BIG = -1e30

F32 = jnp.float32
BF16 = jnp.bfloat16


def _cparams(*sem):
    return pltpu.CompilerParams(dimension_semantics=("arbitrary",) * len(sem), vmem_limit_bytes=VMEM_LIMIT_BYTES)


def _tile(n, pref):
    if n <= pref:
        return n
    while n % pref:
        pref //= 2
    return pref


def _dot(a, b):
    return jnp.dot(a, b, preferred_element_type=F32)


def _dot_nt(a, b):
    return lax.dot_general(a, b, (((1,), (1,)), ((), ())), preferred_element_type=F32)


def _matmul_kernel(a_ref, w_ref, o_ref, acc_ref, *, nk):
    k = pl.program_id(2)

    @pl.when(k == 0)
    def _():
        acc_ref[...] = jnp.zeros_like(acc_ref)

    acc_ref[...] += _dot(a_ref[...].astype(BF16), w_ref[...].astype(BF16))

    @pl.when(k == nk - 1)
    def _():
        o_ref[...] = acc_ref[...].astype(o_ref.dtype)


def _matmul(a, w, out_dtype, tm=2048, tn=1024, tk=1024, name="matmul"):
    m, kd = a.shape
    n = w.shape[1]
    tm, tn, tk = _tile(m, tm), _tile(n, tn), _tile(kd, tk)
    return pl.pallas_call(
        functools.partial(_matmul_kernel, nk=kd // tk),
        grid=(m // tm, n // tn, kd // tk),
        in_specs=[pl.BlockSpec((tm, tk), lambda i, j, k: (i, k)),
                  pl.BlockSpec((tk, tn), lambda i, j, k: (k, j))],
        out_specs=pl.BlockSpec((tm, tn), lambda i, j, k: (i, j)),
        out_shape=jax.ShapeDtypeStruct((m, n), out_dtype),
        scratch_shapes=[pltpu.VMEM((tm, tn), F32)],
        compiler_params=_cparams("parallel", "parallel", "arbitrary"),
        name=name,
    )(a, w)


def _ada_kernel(c_ref, w_ref, b_ref, o_ref, acc_ref, *, nk):
    k = pl.program_id(1)

    @pl.when(k == 0)
    def _():
        acc_ref[...] = jnp.zeros_like(acc_ref)

    c = c_ref[...]
    a = c * jax.nn.sigmoid(c)
    acc_ref[...] += _dot(a.astype(BF16), w_ref[...].astype(BF16))

    @pl.when(k == nk - 1)
    def _():
        o_ref[...] = acc_ref[...] + b_ref[...]


def _ada_proj(c_pad, w_ada, b_ada, tn=2048, tk=512):
    m, kd = c_pad.shape
    n = w_ada.shape[1]
    tn, tk = min(tn, n), min(tk, kd)
    return pl.pallas_call(
        functools.partial(_ada_kernel, nk=kd // tk),
        grid=(n // tn, kd // tk),
        in_specs=[pl.BlockSpec((m, tk), lambda j, k: (0, k)),
                  pl.BlockSpec((tk, tn), lambda j, k: (k, j)),
                  pl.BlockSpec((1, tn), lambda j, k: (0, j))],
        out_specs=pl.BlockSpec((m, tn), lambda j, k: (0, j)),
        out_shape=jax.ShapeDtypeStruct((m, n), F32),
        scratch_shapes=[pltpu.VMEM((m, tn), F32)],
        compiler_params=_cparams("parallel", "arbitrary"),
        name="ada_proj",
    )(c_pad, w_ada, b_ada.reshape(1, n))


def _modulate_kernel(x_ref, shift_ref, scale_ref, h_ref):
    h_ref[...] = (x_ref[...] * (1.0 + scale_ref[...]) + shift_ref[...]).astype(h_ref.dtype)


def _modulate(x, shift, scale, ts=512):
    b, s, d = x.shape
    ts = min(ts, s)
    row = pl.BlockSpec((None, ts, d), lambda bi, i: (bi, i, 0))
    vec = pl.BlockSpec((None, 1, d), lambda bi, i: (bi, 0, 0))
    return pl.pallas_call(
        _modulate_kernel,
        grid=(b, s // ts),
        in_specs=[row, vec, vec],
        out_specs=row,
        out_shape=jax.ShapeDtypeStruct((b, s, d), BF16),
        compiler_params=_cparams("parallel", "parallel"),
        name="modulate",
    )(x, shift, scale)


def _ln_kernel(x_ref, y_ref, gate_ref, g_ref, b_ref, shift_ref, scale_ref, xo_ref, h_ref, h3_ref, *, alpha, nc, pitch):
    ts = x_ref.shape[0]
    z = alpha * x_ref[...] + gate_ref[...] * y_ref[...].astype(F32)
    zc = z - jnp.mean(z, axis=-1, keepdims=True)
    var = jnp.mean(zc * zc, axis=-1, keepdims=True)
    xn = zc * lax.rsqrt(var + LN_EPS) * g_ref[...] + b_ref[...]
    xo_ref[...] = xn
    hf = xn * (1.0 + scale_ref[...]) + shift_ref[...]
    h_ref[...] = hf.astype(h_ref.dtype)
    _slab_store(h3_ref, hf, ts, nc, pitch)


def _ln_residual(x, y, gate, g, b, alpha, shift, scale, nc, pitch, ts=256):
    bsz, s, d = x.shape
    ts = _tile(s, ts)
    nsb = s // ts
    row = pl.BlockSpec((None, ts, d), lambda bi, i: (bi, i, 0))
    vec = pl.BlockSpec((None, 1, d), lambda bi, i: (bi, 0, 0))
    par = pl.BlockSpec((1, d), lambda bi, i: (0, 0))
    return pl.pallas_call(
        functools.partial(_ln_kernel, alpha=alpha, nc=nc, pitch=pitch),
        grid=(bsz, nsb),
        in_specs=[row, row, vec, par, par, vec, vec],
        out_specs=[row, row, pl.BlockSpec((ts * pitch, HEAD_DIM), lambda bi, i: (bi * nsb + i, 0))],
        out_shape=[jax.ShapeDtypeStruct((bsz, s, d), F32),
                   jax.ShapeDtypeStruct((bsz, s, d), BF16),
                   jax.ShapeDtypeStruct((bsz * s * pitch, HEAD_DIM), U32)],
        compiler_params=_cparams("arbitrary", "arbitrary"),
        name="ln_mixer",
    )(x, y, gate, g.reshape(1, d), b.reshape(1, d), shift, scale)


def _prep_kernel(x_ref, gain_ref, cos_ref, sin_ref, *o_refs, flags, half, dests):
    cos = cos_ref[...]
    sin = sin_ref[...]
    lane = lax.broadcasted_iota(jnp.int32, cos.shape, 1)
    first_half = (lane % (2 * half)) < half
    for j, flag in enumerate(flags):
        sl = slice(j * HEAD_DIM, (j + 1) * HEAD_DIM)
        o_ref = o_refs[dests[j][0]]
        osl = slice(dests[j][1] * HEAD_DIM, (dests[j][1] + 1) * HEAD_DIM)
        x = x_ref[:, sl].astype(F32)
        if flag & 1:
            x = x * lax.rsqrt(jnp.mean(x * x, axis=-1, keepdims=True) + RMS_EPS) * gain_ref[j:j + 1, :]
        if flag & 2:
            if 2 * half == HEAD_DIM:
                partner = pltpu.roll(x, half, 1)
            else:
                up = pltpu.roll(x, HEAD_DIM - half, 1)
                dn = pltpu.roll(x, half, 1)
                partner = jnp.where(first_half, up, dn)
            x = x * cos + partner * sin
        o_ref[:, osl] = x.astype(o_ref.dtype)


def _qk_prep(qkv, flags, gains, cos_t, sin_t, half, seq, groups=None, ts=256):
    t, n = qkv.shape
    nc = n // HEAD_DIM
    ts = _tile(seq, ts)
    nsb = seq // ts
    groups = [list(range(nc))] if groups is None else groups
    dests = {j: (gi, jj) for gi, grp in enumerate(groups) for jj, j in enumerate(grp)}
    outs = pl.pallas_call(
        functools.partial(_prep_kernel, flags=tuple(flags), half=half, dests=dests),
        grid=(t // ts,),
        in_specs=[pl.BlockSpec((ts, n), lambda i: (i, 0)),
                  pl.BlockSpec((nc, HEAD_DIM), lambda i: (0, 0)),
                  pl.BlockSpec((ts, HEAD_DIM), lambda i: (i % nsb, 0)),
                  pl.BlockSpec((ts, HEAD_DIM), lambda i: (i % nsb, 0))],
        out_specs=[pl.BlockSpec((ts, len(grp) * HEAD_DIM), lambda i: (i, 0)) for grp in groups],
        out_shape=[jax.ShapeDtypeStruct((t, len(grp) * HEAD_DIM), BF16) for grp in groups],
        compiler_params=_cparams("arbitrary"),
        name="qk_prep",
    )(qkv, gains, cos_t, sin_t)
    return outs[0] if len(groups) == 1 else outs


def _rope_tables(pos, dim):
    inv = ROPE_THETA ** (-jnp.arange(0, dim, 2, dtype=F32) / dim)
    ang = pos.astype(F32)[:, None] * inv[None, :]
    return jnp.cos(ang), jnp.sin(ang)


def _softmax_pv(s, v):
    m = jnp.max(s, axis=-1, keepdims=True)
    p = jnp.exp(s - m)
    l = jnp.sum(p, axis=-1, keepdims=True)
    return _dot(p.astype(BF16), v) / l


def _dense_head(q, k, v, scale):
    return _softmax_pv(_dot_nt(q, k) * scale, v)


def _gqa_attn_kernel(q_ref, k_ref, v_ref, o_ref, *, n_q, scale):
    k = k_ref[...]
    v = v_ref[...]
    for j in range(n_q):
        sl = slice(j * HEAD_DIM, (j + 1) * HEAD_DIM)
        o_ref[:, sl] = _dense_head(q_ref[:, sl], k, v, scale).astype(o_ref.dtype)


def _diff_attn_kernel(q_ref, k_ref, v_ref, lam_ref, gain_ref, o_ref, *, n_g, scale, lam_init):
    lp = lam_ref[...]
    lam = (jnp.exp(jnp.sum(lp[0:1] * lp[1:2], axis=-1, keepdims=True))
           - jnp.exp(jnp.sum(lp[2:3] * lp[3:4], axis=-1, keepdims=True)) + lam_init)
    v = v_ref[...]
    dv = v.shape[-1]
    for g in range(n_g):
        o = None
        for c in range(2):
            j = 2 * g + c
            q = q_ref[:, j * HEAD_DIM:(j + 1) * HEAD_DIM]
            oc = _dense_head(q, k_ref[:, c * HEAD_DIM:(c + 1) * HEAD_DIM], v, scale)
            o = oc if c == 0 else o - lam * oc
        o = o * lax.rsqrt(jnp.mean(o * o, axis=-1, keepdims=True) + RMS_EPS) * gain_ref[...]
        o_ref[:, g * dv:(g + 1) * dv] = (o * (1.0 - lam_init)).astype(o_ref.dtype)


def _gqa_attention(qkv, bsz, seq, n_kv, tq=256):
    g = GQA_RATIO
    n_h = n_kv * g
    tq = min(tq, seq)
    nq = seq // tq
    qw = g * HEAD_DIM
    return pl.pallas_call(
        functools.partial(_gqa_attn_kernel, n_q=g, scale=HEAD_DIM ** -0.5),
        grid=(bsz, n_kv, nq),
        in_specs=[pl.BlockSpec((tq, qw), lambda b, h, i: (b * nq + i, h)),
                  pl.BlockSpec((seq, HEAD_DIM), lambda b, h, i: (b, n_h + h)),
                  pl.BlockSpec((seq, HEAD_DIM), lambda b, h, i: (b, n_h + n_kv + h))],
        out_specs=pl.BlockSpec((tq, qw), lambda b, h, i: (b * nq + i, h)),
        out_shape=jax.ShapeDtypeStruct((bsz * seq, n_h * HEAD_DIM), BF16),
        compiler_params=_cparams("parallel", "parallel", "arbitrary"),
        name="gqa_attention",
    )(qkv, qkv, qkv)


def _diff_attention(qkv, lam_params, sub_gain, lam_init, bsz, seq, n_kv, tq=256):
    g = GQA_RATIO
    n_h = n_kv * g
    tq = min(tq, seq)
    nq = seq // tq
    qw = g * 2 * HEAD_DIM
    dv = 2 * HEAD_DIM
    k_blk0 = (2 * n_h * HEAD_DIM) // dv
    v_blk0 = k_blk0 + n_kv
    return pl.pallas_call(
        functools.partial(_diff_attn_kernel, n_g=g, scale=HEAD_DIM ** -0.5, lam_init=lam_init),
        grid=(bsz, n_kv, nq),
        in_specs=[pl.BlockSpec((tq, qw), lambda b, h, i: (b * nq + i, h)),
                  pl.BlockSpec((seq, dv), lambda b, h, i: (b, k_blk0 + h)),
                  pl.BlockSpec((seq, dv), lambda b, h, i: (b, v_blk0 + h)),
                  pl.BlockSpec((4, HEAD_DIM), lambda b, h, i: (0, 0)),
                  pl.BlockSpec((1, dv), lambda b, h, i: (0, 0))],
        out_specs=pl.BlockSpec((tq, g * dv), lambda b, h, i: (b * nq + i, h)),
        out_shape=jax.ShapeDtypeStruct((bsz * seq, n_h * dv), BF16),
        compiler_params=_cparams("parallel", "parallel", "arbitrary"),
        name="diff_attention",
    )(qkv, qkv, qkv, lam_params.astype(F32), sub_gain.reshape(1, dv).astype(F32))


def _dil_attn_kernel(q_ref, k_ref, v_ref, o_ref, st_ref, *, n_g, half, sub, scale):
    tl = q_ref.shape[0]
    length = k_ref.shape[0]
    win = min(length, sub + 2 * half)
    i = pl.program_id(3)
    lane = lax.broadcasted_iota(jnp.int32, (sub, HEAD_DIM), 1)
    for sb in range(tl // sub):
        qs = i * tl + sb * sub
        start = jnp.clip(qs - half, 0, length - win)
        start = pl.multiple_of(start, half)
        kw = k_ref[pl.ds(start, win), :]
        vw = v_ref[pl.ds(start, win), :]
        kpos = start + lax.broadcasted_iota(jnp.int32, (sub, win), 1)
        qpos = qs + lax.broadcasted_iota(jnp.int32, (sub, win), 0)
        valid = jnp.abs(kpos - qpos) <= half
        stats = jnp.zeros((sub, HEAD_DIM), F32)
        for g in range(n_g):
            sl = slice(g * HEAD_DIM, (g + 1) * HEAD_DIM)
            s = _dot_nt(q_ref[sb * sub:(sb + 1) * sub, sl], kw) * scale
            s = jnp.where(valid, s, NEG_BIG)
            m = jnp.max(s, axis=-1, keepdims=True)
            p = jnp.exp(s - m)
            l = jnp.sum(p, axis=-1, keepdims=True)
            o_ref[sb * sub:(sb + 1) * sub, sl] = (_dot(p.astype(BF16), vw) / l).astype(o_ref.dtype)
            stats = jnp.where(lane == g, m, stats)
            stats = jnp.where(lane == n_g + g, l, stats)
        st_ref[sb * sub:(sb + 1) * sub, :] = stats


def _dilated_group(qkv, window, dilation, bsz, seq, n_kv, tl=512, sub=128):
    g = GQA_RATIO
    n_h = n_kv * g
    d = dilation
    length = seq // d
    half = window // (2 * d)
    sub = min(sub, length)
    tl = min(tl, length)
    nl = length // tl
    n_cols = qkv.shape[1]
    qkv_v = qkv.reshape(bsz * length, d * n_cols)
    qw = g * HEAD_DIM
    assert n_cols % qw == 0 and length % tl == 0 and tl % sub == 0
    q_blk = lambda b, r, h, i: (b * nl + i, (r * n_cols) // qw + h)
    k_col0 = n_h * HEAD_DIM
    v_col0 = (n_h + n_kv) * HEAD_DIM
    k_blk = lambda b, r, h, i: (b, (r * n_cols + k_col0) // HEAD_DIM + h)
    v_blk = lambda b, r, h, i: (b, (r * n_cols + v_col0) // HEAD_DIM + h)
    o, st = pl.pallas_call(
        functools.partial(_dil_attn_kernel, n_g=g, half=half, sub=sub, scale=HEAD_DIM ** -0.5),
        grid=(bsz, d, n_kv, nl),
        in_specs=[pl.BlockSpec((tl, qw), q_blk),
                  pl.BlockSpec((length, HEAD_DIM), k_blk),
                  pl.BlockSpec((length, HEAD_DIM), v_blk)],
        out_specs=[pl.BlockSpec((tl, qw), lambda b, r, h, i: (b * nl + i, r * n_kv + h)),
                   pl.BlockSpec((tl, HEAD_DIM), lambda b, r, h, i: (b * nl + i, r * n_kv + h))],
        out_shape=[jax.ShapeDtypeStruct((bsz * length, d * n_h * HEAD_DIM), F32),
                   jax.ShapeDtypeStruct((bsz * length, d * n_kv * HEAD_DIM), F32)],
        compiler_params=_cparams("parallel", "parallel", "parallel", "arbitrary"),
        name=f"dilated_attention_d{d}",
    )(qkv_v, qkv_v, qkv_v)
    return o.reshape(bsz * seq, n_h * HEAD_DIM), st.reshape(bsz * seq, n_kv * HEAD_DIM)


def _dil_merge_kernel(*refs, n_grp, n_kv, n_g):
    o_refs, st_refs, out_ref = refs[:n_grp], refs[n_grp:2 * n_grp], refs[2 * n_grp]
    for kv in range(n_kv):
        st = [r[:, kv * HEAD_DIM:(kv + 1) * HEAD_DIM] for r in st_refs]
        for g in range(n_g):
            h = kv * n_g + g
            ms = [s[:, g:g + 1] for s in st]
            ls = [s[:, n_g + g:n_g + g + 1] for s in st]
            mmax = functools.reduce(jnp.maximum, ms)
            ws = [l * jnp.exp(m - mmax) for m, l in zip(ms, ls)]
            wsum = functools.reduce(lambda a, b: a + b, ws)
            sl = slice(h * HEAD_DIM, (h + 1) * HEAD_DIM)
            acc = None
            for w, o_ref in zip(ws, o_refs):
                term = (w / wsum) * o_ref[:, sl]
                acc = term if acc is None else acc + term
            out_ref[:, sl] = acc.astype(out_ref.dtype)


def _dilated_merge(outs, stats, n_kv, ts=256):
    t, n = outs[0].shape
    ts = min(ts, t)
    n_grp = len(outs)
    o_spec = pl.BlockSpec((ts, n), lambda i: (i, 0))
    s_spec = pl.BlockSpec((ts, stats[0].shape[1]), lambda i: (i, 0))
    return pl.pallas_call(
        functools.partial(_dil_merge_kernel, n_grp=n_grp, n_kv=n_kv, n_g=GQA_RATIO),
        grid=(t // ts,),
        in_specs=[o_spec] * n_grp + [s_spec] * n_grp,
        out_specs=o_spec,
        out_shape=jax.ShapeDtypeStruct((t, n), BF16),
        compiler_params=_cparams("parallel"),
        name="dilated_merge",
    )(*outs, *stats)


def _nat_attn_kernel(q_ref, k_ref, v_ref, bias_ref, o_ref, *, n_g, rows, scale):
    rows_per_step = q_ref.shape[0] // GRID_W
    i = pl.program_id(2)
    nk = NAT_WIN_ROWS * GRID_W
    for rr in range(rows_per_step):
        r = i * rows_per_step + rr
        rs = jnp.clip(r - NAT_WIN_ROWS // 2, 0, rows - NAT_WIN_ROWS)
        ro = r - rs
        start = pl.multiple_of(rs * GRID_W, GRID_W)
        kw = k_ref[pl.ds(start, nk), :]
        vw = v_ref[pl.ds(start, nk), :]
        rsl = slice(rr * GRID_W, (rr + 1) * GRID_W)
        q = jnp.concatenate([q_ref[rsl, g * HEAD_DIM:(g + 1) * HEAD_DIM] for g in range(n_g)], axis=0)
        s = _dot_nt(q, kw) * scale + bias_ref[ro]
        o = _softmax_pv(s, vw)
        for g in range(n_g):
            o_ref[rsl, g * HEAD_DIM:(g + 1) * HEAD_DIM] = o[g * GRID_W:(g + 1) * GRID_W].astype(o_ref.dtype)


def _nat_bias_table(rpb, n_kv):
    n_h = rpb.shape[0]
    g = n_h // n_kv
    nr, ncol, w = NAT_WIN_ROWS, NAT_WIN_COLS, GRID_W
    ro, ki = np.meshgrid(np.arange(nr), np.arange(nr), indexing="ij")
    rowsel = np.zeros((nr, nr, 2 * nr - 1), np.float32)
    rowsel[ro, ki, ki - ro + nr - 1] = 1.0
    c, kc = np.meshgrid(np.arange(w), np.arange(w), indexing="ij")
    cs = np.clip(c - ncol // 2, 0, w - ncol)
    valid = (kc >= cs) & (kc < cs + ncol)
    colsel = np.zeros((2 * ncol - 1, w, w), np.float32)
    colsel[(kc - c + ncol - 1)[valid], c[valid], kc[valid]] = 1.0
    rows = jnp.einsum("hij,rki->hrkj", rpb.astype(F32), rowsel, precision=lax.Precision.HIGHEST)
    tab = jnp.einsum("hrkj,jcq->hrckq", rows, colsel, precision=lax.Precision.HIGHEST)
    tab = jnp.where(valid[None, None, :, None, :], tab, NEG_BIG)
    tab = tab.reshape(n_kv, g, nr, w, nr * w)
    return tab.transpose(0, 2, 1, 3, 4).reshape(n_kv, nr, g * w, nr * w)


def _nat_attention(qkv, bias_tab, bsz, seq, n_kv, rows_per_step=8):
    g = GQA_RATIO
    n_h = n_kv * g
    rows = seq // GRID_W
    assert rows >= NAT_WIN_ROWS
    rows_per_step = min(rows_per_step, rows)
    tq = rows_per_step * GRID_W
    nq = seq // tq
    qw = g * HEAD_DIM
    return pl.pallas_call(
        functools.partial(_nat_attn_kernel, n_g=g, rows=rows, scale=HEAD_DIM ** -0.5),
        grid=(bsz, n_kv, nq),
        in_specs=[pl.BlockSpec((tq, qw), lambda b, h, i: (b * nq + i, h)),
                  pl.BlockSpec((seq, HEAD_DIM), lambda b, h, i: (b, n_h + h)),
                  pl.BlockSpec((seq, HEAD_DIM), lambda b, h, i: (b, n_h + n_kv + h)),
                  pl.BlockSpec((None,) + bias_tab.shape[1:], lambda b, h, i: (h, 0, 0, 0))],
        out_specs=pl.BlockSpec((tq, qw), lambda b, h, i: (b * nq + i, h)),
        out_shape=jax.ShapeDtypeStruct((bsz * seq, n_h * HEAD_DIM), BF16),
        compiler_params=_cparams("parallel", "parallel", "arbitrary"),
        name="nat_attention",
    )(qkv, qkv, qkv, bias_tab)


def _router_kernel(h_ref, w_ref, bias_ref, e_ref, wt_ref, rank_ref, cnt_ref, carry_ref, *, n_e):
    i = pl.program_id(0)
    tm = h_ref.shape[0]
    per_group = n_e // N_EXPERT_GROUPS

    @pl.when(i == 0)
    def _():
        carry_ref[...] = jnp.zeros_like(carry_ref)

    h = h_ref[...]
    w = w_ref[...]
    w_hi = w.astype(BF16)
    w_lo = (w - w_hi.astype(F32)).astype(BF16)
    logits = _dot_nt(w_hi, h) + _dot_nt(w_lo, h)
    scores = jax.nn.sigmoid(logits)
    biased = scores + bias_ref[...]

    b3 = biased.reshape(N_EXPERT_GROUPS, per_group, tm)
    idx3 = lax.broadcasted_iota(jnp.int32, b3.shape, 1)
    m1 = jnp.max(b3, axis=1, keepdims=True)
    i1 = jnp.min(jnp.where(b3 == m1, idx3, per_group), axis=1, keepdims=True)
    m2 = jnp.max(jnp.where(idx3 == i1, -jnp.inf, b3), axis=1, keepdims=True)
    gs = (m1 + m2).reshape(N_EXPERT_GROUPS, tm)

    gidx = lax.broadcasted_iota(jnp.int32, gs.shape, 0)
    gsel = jnp.zeros(gs.shape, jnp.bool_)
    for _ in range(TOPK_GROUPS):
        gm = jnp.max(gs, axis=0, keepdims=True)
        gi = jnp.min(jnp.where(gs == gm, gidx, N_EXPERT_GROUPS), axis=0, keepdims=True)
        hit = gidx == gi
        gsel = gsel | hit
        gs = jnp.where(hit, -jnp.inf, gs)
    emask = jnp.broadcast_to(gsel.reshape(N_EXPERT_GROUPS, 1, tm), b3.shape).reshape(n_e, tm)
    masked = jnp.where(emask, biased, -jnp.inf)

    eidx = lax.broadcasted_iota(jnp.int32, masked.shape, 0)
    sel = jnp.zeros(masked.shape, jnp.bool_)
    top_e, top_w, hits = [], [], []
    for _ in range(TOP_K):
        mx = jnp.max(masked, axis=0, keepdims=True)
        ei = jnp.min(jnp.where(masked == mx, eidx, n_e), axis=0, keepdims=True)
        hit = eidx == ei
        top_e.append(ei)
        top_w.append(jnp.sum(jnp.where(hit, scores, 0.0), axis=0, keepdims=True))
        hits.append(hit)
        sel = sel | hit
        masked = jnp.where(hit, -jnp.inf, masked)
    wsum = functools.reduce(lambda a, b: a + b, top_w)

    selb = jnp.where(sel, 1.0, 0.0).astype(BF16)
    tri = (lax.broadcasted_iota(jnp.int32, (tm, tm), 0) < lax.broadcasted_iota(jnp.int32, (tm, tm), 1))
    ranks = _dot(selb, jnp.where(tri, 1.0, 0.0).astype(BF16)) + carry_ref[:, 0:1]
    carry_ref[...] = carry_ref[...] + jnp.sum(selb.astype(F32), axis=1, keepdims=True)

    for k in range(TOP_K):
        e_ref[k:k + 1, :] = top_e[k]
        wt_ref[k:k + 1, :] = top_w[k] / wsum * ROUTED_SCALE
        rank_ref[k:k + 1, :] = jnp.sum(jnp.where(hits[k], ranks, 0.0), axis=0, keepdims=True).astype(jnp.int32)
    cnt_ref[...] = carry_ref[...]


def _router(x, router_w_t, router_bias, tm=512):
    t, d = x.shape
    n_e = router_w_t.shape[0]
    tm = min(tm, t)
    tok = pl.BlockSpec((TOP_K, tm), lambda i: (0, i))
    e, w, rank, cnt = pl.pallas_call(
        functools.partial(_router_kernel, n_e=n_e),
        grid=(t // tm,),
        in_specs=[pl.BlockSpec((tm, d), lambda i: (i, 0)),
                  pl.BlockSpec((n_e, d), lambda i: (0, 0)),
                  pl.BlockSpec((n_e, 1), lambda i: (0, 0))],
        out_specs=[tok, tok, tok, pl.BlockSpec((n_e, HEAD_DIM), lambda i: (0, 0))],
        out_shape=[jax.ShapeDtypeStruct((TOP_K, t), jnp.int32),
                   jax.ShapeDtypeStruct((TOP_K, t), F32),
                   jax.ShapeDtypeStruct((TOP_K, t), jnp.int32),
                   jax.ShapeDtypeStruct((n_e, HEAD_DIM), F32)],
        scratch_shapes=[pltpu.VMEM((n_e, HEAD_DIM), F32)],
        compiler_params=_cparams("arbitrary"),
        name="moe_router",
    )(x, router_w_t, router_bias.reshape(n_e, 1).astype(F32))
    return e, w, rank, cnt[:, 0]


U32 = jnp.uint32
HI_MASK = 0xFFFF0000


def _slab_pitch(nc):
    pitch = nc + 8
    return pitch if (pitch // 8) % 2 else pitch + 8


def _bf16_bits(x):
    return lax.bitcast_convert_type(x.astype(BF16).astype(F32), U32)


def _slab_words(ref, c, n_tok, pitch, row0=0):
    w = ref[pl.ds(row0 + c, n_tok, stride=pitch), :]
    lo = lax.bitcast_convert_type(w << 16, F32)
    hi = lax.bitcast_convert_type(w & U32(HI_MASK), F32)
    return lo, hi


def _slab_load(ref, n_tok, nc, pitch):
    halves = [_slab_words(ref, c, n_tok, pitch) for c in range(nc)]
    return jnp.concatenate([lo.astype(BF16) for lo, _ in halves] + [hi.astype(BF16) for _, hi in halves], axis=1)


def _slab_store(ref, val, n_tok, nc, pitch):
    half = nc * HEAD_DIM
    for c in range(nc):
        lo = _bf16_bits(val[:, c * HEAD_DIM:(c + 1) * HEAD_DIM]) >> 16
        hi = _bf16_bits(val[:, half + c * HEAD_DIM:half + (c + 1) * HEAD_DIM]) & U32(HI_MASK)
        ref[pl.ds(c, n_tok, stride=pitch), :] = lo | hi
    for r in range(nc, pitch):
        ref[pl.ds(r, n_tok, stride=pitch), :] = jnp.zeros((n_tok, HEAD_DIM), U32)


def _gather_pipeline(i, n_act, idx_hbm, src_hbm, idx_smem, buf, idx_sem, g_sem, *, n_idx, nc, pitch):
    slot = i % 2
    rows = n_idx * nc

    def idx_copy(tile, sl):
        return pltpu.make_async_copy(idx_hbm.at[pl.ds(tile * n_idx, n_idx)],
                                     idx_smem.at[pl.ds(sl * n_idx, n_idx)], idx_sem.at[sl])

    def issue(sl):
        def body(j, carry):
            row0 = idx_smem[sl * n_idx + j]
            pltpu.make_async_copy(src_hbm.at[pl.ds(pl.multiple_of(row0, 8), nc), :],
                                  buf.at[sl, pl.ds(pl.multiple_of(j * pitch, 8), nc), :], g_sem.at[sl]).start()
            return carry
        lax.fori_loop(0, n_idx, body, 0, unroll=16)

    @pl.when(i == 0)
    def _():
        idx_copy(0, 0).start()
        idx_copy(0, 0).wait()
        issue(0)

        @pl.when(1 < n_act)
        def _():
            idx_copy(1, 1).start()

    @pl.when(i + 1 < n_act)
    def _():
        idx_copy(i + 1, 1 - slot).wait()
        issue(1 - slot)

    @pl.when(i + 2 < n_act)
    def _():
        idx_copy(i + 2, slot).start()

    pltpu.make_async_copy(src_hbm.at[pl.ds(0, rows), :], buf.at[slot, pl.ds(0, rows), :], g_sem.at[slot]).wait()


def _experts_kernel(te_ref, nt_ref, idx_hbm, h3_hbm, wg_ref, wu_ref, wd_ref, ys_ref,
                    idx_smem, xbuf, idx_sem, g_sem, *, tm, nc, pitch):
    i = pl.program_id(0)
    nt = nt_ref[0]

    @pl.when(i < nt)
    def _():
        _gather_pipeline(i, nt, idx_hbm, h3_hbm, idx_smem, xbuf, idx_sem, g_sem, n_idx=tm, nc=nc, pitch=pitch)
        x = _slab_load(xbuf.at[i % 2], tm, nc, pitch)
        a = _dot(x, wg_ref[...])
        u = _dot(x, wu_ref[...])
        hmid = (a * jax.nn.sigmoid(a) * u).astype(BF16)
        y = _dot(hmid, wd_ref[...])
        _slab_store(ys_ref, y, tm, nc, pitch)

    @pl.when(i >= nt)
    def _():
        ys_ref[...] = jnp.zeros_like(ys_ref)


def _routed_experts(h3, src_tok, tile_expert, n_tiles, w_gate, w_up, w_down, layer, tm, nc, pitch):
    n_tiles_max = tile_expert.shape[0]
    d = w_gate.shape[-2]
    hid = w_gate.shape[-1]
    return pl.pallas_call(
        functools.partial(_experts_kernel, tm=tm, nc=nc, pitch=pitch),
        grid_spec=pltpu.PrefetchScalarGridSpec(
            num_scalar_prefetch=2,
            grid=(n_tiles_max,),
            in_specs=[pl.BlockSpec(memory_space=pl.ANY),
                      pl.BlockSpec(memory_space=pl.ANY),
                      pl.BlockSpec((None, None, d, hid), lambda i, te, nt: (layer, te[i], 0, 0)),
                      pl.BlockSpec((None, None, d, hid), lambda i, te, nt: (layer, te[i], 0, 0)),
                      pl.BlockSpec((None, None, hid, d), lambda i, te, nt: (layer, te[i], 0, 0))],
            out_specs=pl.BlockSpec((tm * pitch, HEAD_DIM), lambda i, te, nt: (i, 0)),
            scratch_shapes=[pltpu.SMEM((2 * tm,), jnp.int32),
                            pltpu.VMEM((2, tm * pitch, HEAD_DIM), U32),
                            pltpu.SemaphoreType.DMA((2,)),
                            pltpu.SemaphoreType.DMA((2,))],
        ),
        out_shape=jax.ShapeDtypeStruct((n_tiles_max * tm * pitch, HEAD_DIM), U32),
        compiler_params=_cparams("arbitrary"),
        name="moe_experts",
    )(tile_expert, n_tiles, src_tok, h3, w_gate, w_up, w_down)


def _ffn_kernel(x_ref, wg_ref, wu_ref, wd_ref, o_ref):
    x = x_ref[...]
    a = _dot(x, wg_ref[...])
    u = _dot(x, wu_ref[...])
    hmid = (a * jax.nn.sigmoid(a) * u).astype(BF16)
    o_ref[...] = _dot(hmid, wd_ref[...]).astype(o_ref.dtype)


def _shared_ffn(h, w_gate, w_up, w_down, layer, tm=1024):
    t, d = h.shape
    hid = w_gate.shape[-1]
    tm = _tile(t, tm)
    return pl.pallas_call(
        _ffn_kernel,
        grid=(t // tm,),
        in_specs=[pl.BlockSpec((tm, d), lambda i: (i, 0)),
                  pl.BlockSpec((None, d, hid), lambda i: (layer, 0, 0)),
                  pl.BlockSpec((None, d, hid), lambda i: (layer, 0, 0)),
                  pl.BlockSpec((None, hid, d), lambda i: (layer, 0, 0))],
        out_specs=pl.BlockSpec((tm, d), lambda i: (i, 0)),
        out_shape=jax.ShapeDtypeStruct((t, d), BF16),
        compiler_params=_cparams("arbitrary"),
        name="moe_shared",
    )(h, w_gate, w_up, w_down)


def _combine_ln_kernel(*refs, alpha, ts, nc, pitch, n_steps, with_h):
    it = iter(refs)
    idx_hbm, ys_hbm, x_ref, sh_ref, wk_ref, gate_ref, g_ref, b_ref = [next(it) for _ in range(8)]
    if with_h:
        shift_ref, scale_ref = next(it), next(it)
    xo_ref = next(it)
    if with_h:
        h_ref = next(it)
    idx_smem, buf, idx_sem, g_sem = [next(it) for _ in range(4)]

    i = pl.program_id(0)
    _gather_pipeline(i, n_steps, idx_hbm, ys_hbm, idx_smem, buf, idx_sem, g_sem,
                     n_idx=TOP_K * ts, nc=nc, pitch=pitch)
    slabs = buf.at[i % 2]
    wk = wk_ref[...]
    wkb = [jnp.broadcast_to(wk[:, k:k + 1], (ts, HEAD_DIM)) for k in range(TOP_K)]
    lo_pieces, hi_pieces = [], []
    for c in range(nc):
        acc_lo = acc_hi = None
        for k in range(TOP_K):
            lo, hi = _slab_words(slabs, c, ts, pitch, row0=k * ts * pitch)
            acc_lo = wkb[k] * lo if acc_lo is None else acc_lo + wkb[k] * lo
            acc_hi = wkb[k] * hi if acc_hi is None else acc_hi + wkb[k] * hi
        lo_pieces.append(acc_lo)
        hi_pieces.append(acc_hi)
    y = sh_ref[...].astype(F32) + jnp.concatenate(lo_pieces + hi_pieces, axis=1)
    z = alpha * x_ref[...] + gate_ref[...] * y
    zc = z - jnp.mean(z, axis=-1, keepdims=True)
    var = jnp.mean(zc * zc, axis=-1, keepdims=True)
    xn = zc * lax.rsqrt(var + LN_EPS) * g_ref[...] + b_ref[...]
    xo_ref[...] = xn
    if with_h:
        h_ref[...] = (xn * (1.0 + scale_ref[...]) + shift_ref[...]).astype(h_ref.dtype)


def _combine_ln(x, shared, ys, dest, wk, gate, g, b, alpha, nc, pitch, shift=None, scale=None, ts=128):
    bsz, s, d = x.shape
    t = bsz * s
    ts = _tile(s, ts)
    nsb = s // ts
    n_steps = t // ts
    with_h = shift is not None
    idx = (dest * pitch).reshape(TOP_K, n_steps, ts).transpose(1, 0, 2).reshape(-1)
    row = pl.BlockSpec((ts, d), lambda i: (i, 0))
    vec = pl.BlockSpec((None, 1, d), lambda i: (i // nsb, 0, 0))
    par = pl.BlockSpec((1, d), lambda i: (0, 0))
    hbm = pl.BlockSpec(memory_space=pl.ANY)
    args = [idx, ys, x.reshape(t, d), shared, wk, gate, g.reshape(1, d), b.reshape(1, d)]
    specs = [hbm, hbm, row, row, pl.BlockSpec((ts, TOP_K), lambda i: (i, 0)), vec, par, par]
    if with_h:
        args += [shift, scale]
        specs += [vec, vec]
    out_shape = [jax.ShapeDtypeStruct((t, d), F32)]
    out_specs = [row]
    if with_h:
        out_shape.append(jax.ShapeDtypeStruct((t, d), BF16))
        out_specs.append(row)
    outs = pl.pallas_call(
        functools.partial(_combine_ln_kernel, alpha=alpha, ts=ts, nc=nc, pitch=pitch, n_steps=n_steps, with_h=with_h),
        grid=(n_steps,),
        in_specs=specs,
        out_specs=out_specs,
        out_shape=out_shape,
        scratch_shapes=[pltpu.SMEM((2 * TOP_K * ts,), jnp.int32),
                        pltpu.VMEM((2, TOP_K * ts * pitch, HEAD_DIM), U32),
                        pltpu.SemaphoreType.DMA((2,)),
                        pltpu.SemaphoreType.DMA((2,))],
        compiler_params=_cparams("arbitrary"),
        name="moe_combine_ln",
    )(*args)
    x_new = outs[0].reshape(bsz, s, d)
    return (x_new, outs[1].reshape(bsz, s, d)) if with_h else (x_new, None)


def _moe(h, h3, router_w, router_bias, w_gate, w_up, w_down, s_gate, s_up, s_down, layer, nc, pitch, tm=256):
    t, d = h.shape
    n_e = router_w.shape[1]
    tm = min(tm, t)
    top_e, top_w, rank, counts = _router(h, router_w.T.astype(F32), router_bias)

    counts = counts.astype(jnp.int32)
    padded = ((counts + tm - 1) // tm) * tm
    ends = jnp.cumsum(padded)
    offsets = ends - padded
    n_tiles_max = (t * TOP_K + n_e * (tm - 1)) // tm
    n_tiles = (ends[-1] // tm).astype(jnp.int32).reshape(1)
    tile_start = jnp.arange(n_tiles_max, dtype=jnp.int32) * tm
    tile_expert = jnp.minimum(jnp.sum((ends[None, :] <= tile_start[:, None]).astype(jnp.int32), axis=1), n_e - 1)
    onehot = top_e[:, :, None] == jnp.arange(n_e, dtype=jnp.int32)[None, None, :]
    dest = rank + jnp.sum(jnp.where(onehot, offsets[None, None, :], 0), axis=-1)
    tok_row0 = jnp.broadcast_to(jnp.arange(t, dtype=jnp.int32)[None, :] * pitch, dest.shape)
    src_row0 = jnp.zeros((n_tiles_max * tm,), jnp.int32).at[dest.reshape(-1)].set(
        tok_row0.reshape(-1), unique_indices=True)

    ys = _routed_experts(h3, src_row0, tile_expert, n_tiles, w_gate, w_up, w_down, layer, tm, nc, pitch)
    shared = _shared_ffn(h, s_gate, s_up, s_down, layer)
    return shared, ys, dest, top_w.T


def kernel(x, c, w_ada, b_ada, ada_table, ln_gain, ln_bias, gqa_wqkv, gqa_wo, gqa_q_gain, gqa_k_gain, dil_wqkv, dil_wo, nat_wqkv, nat_wo, nat_rpb, dif_wqkv, dif_wo, dif_lambda, dif_subln_gain, router_w, router_bias, exp_w_gate, exp_w_up, exp_w_down, sh_w_gate, sh_w_up, sh_w_down):
    bsz, seq, d = x.shape
    depth = ada_table.shape[0]
    t = bsz * seq
    alpha = (2 * depth) ** 0.25
    n_h = d // HEAD_DIM
    n_kv = n_h // GQA_RATIO
    n_h2 = d // (2 * HEAD_DIM)
    n_kv2 = n_h2 // GQA_RATIO
    n_grp = len(DIL_PATTERNS)
    nc = d // (2 * HEAD_DIM)
    pitch = _slab_pitch(nc)

    pos = jnp.arange(seq)
    cos1, sin1 = _rope_tables(pos, HEAD_DIM)
    cos1_t = jnp.concatenate([cos1, cos1], axis=-1)
    sin1_t = jnp.concatenate([-sin1, sin1], axis=-1)
    cr, sr = _rope_tables(pos // GRID_W, HEAD_DIM // 2)
    cc, sc = _rope_tables(pos % GRID_W, HEAD_DIM // 2)
    cos2_t = jnp.concatenate([cr, cr, cc, cc], axis=-1)
    sin2_t = jnp.concatenate([-sr, sr, -sc, sc], axis=-1)

    pad_rows = 16
    c_pad = jnp.zeros((pad_rows, d), F32).at[:bsz].set(c.astype(F32))
    mod_shared = _ada_proj(c_pad, w_ada, b_ada)[:bsz].reshape(bsz, N_MOD, d)

    def mod_vec(mod, j):
        return mod[:, j:j + 1, :]

    expert_w = [w.astype(BF16) for w in (exp_w_gate, exp_w_up, exp_w_down)]
    shared_w = [w.astype(BF16) for w in (sh_w_gate, sh_w_up, sh_w_down)]

    mod = mod_shared + ada_table[0]
    h = _modulate(x, mod_vec(mod, 0), mod_vec(mod, 1))
    for i in range(depth):
        kind, j = i % N_MIXERS, i // N_MIXERS
        mod = mod_shared + ada_table[i]
        h2 = h.reshape(t, d)
        if kind == 0:
            qkv = _matmul(h2, gqa_wqkv[j], F32, name="gqa_qkv")
            flags = [3] * (n_h + n_kv) + [0] * n_kv
            gains = jnp.concatenate([jnp.broadcast_to(gqa_q_gain[j], (n_h, HEAD_DIM)),
                                     jnp.broadcast_to(gqa_k_gain[j], (n_kv, HEAD_DIM)),
                                     jnp.ones((n_kv, HEAD_DIM), F32)]).astype(F32)
            qkv = _qk_prep(qkv, flags, gains, cos2_t, sin2_t, HEAD_DIM // 4, seq)
            o = _gqa_attention(qkv, bsz, seq, n_kv)
            wo = gqa_wo[j]
        elif kind == 1:
            qkv = _matmul(h2, dil_wqkv[j], F32, name="dil_qkv")
            n_qk = n_grp * (n_h2 + n_kv2)
            flags = [2] * n_qk + [0] * (n_grp * n_kv2)
            gains = jnp.ones((n_qk + n_grp * n_kv2, HEAD_DIM), F32)
            k0, v0 = n_grp * n_h2, n_grp * (n_h2 + n_kv2)
            groups = [list(range(grp * n_h2, (grp + 1) * n_h2))
                      + list(range(k0 + grp * n_kv2, k0 + (grp + 1) * n_kv2))
                      + list(range(v0 + grp * n_kv2, v0 + (grp + 1) * n_kv2)) for grp in range(n_grp)]
            qkv_g = _qk_prep(qkv, flags, gains, cos1_t, sin1_t, HEAD_DIM // 2, seq, groups=groups)
            outs, stats = [], []
            for grp, (window, dilation) in enumerate(DIL_PATTERNS):
                og, sg = _dilated_group(qkv_g[grp], window, dilation, bsz, seq, n_kv2)
                outs.append(og)
                stats.append(sg)
            o = _dilated_merge(outs, stats, n_kv2)
            wo = dil_wo[j]
        elif kind == 2:
            qkv = _matmul(h2, nat_wqkv[j], BF16, name="nat_qkv")
            o = _nat_attention(qkv, _nat_bias_table(nat_rpb[j], n_kv), bsz, seq, n_kv)
            wo = nat_wo[j]
        else:
            qkv = _matmul(h2, dif_wqkv[j], F32, name="dif_qkv")
            n_qk = 2 * n_h2 + 2 * n_kv2
            flags = [2] * n_qk + [0] * (2 * n_kv2)
            gains = jnp.ones((n_qk + 2 * n_kv2, HEAD_DIM), F32)
            qkv = _qk_prep(qkv, flags, gains, cos1_t, sin1_t, HEAD_DIM // 2, seq)
            lam_init = 0.8 - 0.6 * math.exp(-0.3 * i)
            o = _diff_attention(qkv, dif_lambda[j], dif_subln_gain[j], lam_init, bsz, seq, n_kv2)
            wo = dif_wo[j]
        y = _matmul(o, wo, BF16, name="mixer_out").reshape(bsz, seq, d)
        x, h, h3 = _ln_residual(x, y, mod_vec(mod, 2), ln_gain[i, 0], ln_bias[i, 0], alpha,
                                mod_vec(mod, 3), mod_vec(mod, 4), nc, pitch)

        shared, ys, dest, wk = _moe(h.reshape(t, d), h3, router_w[i], router_bias[i],
                                    *expert_w, *shared_w, i, nc, pitch)
        if i + 1 < depth:
            nxt = mod_shared + ada_table[i + 1]
            shift_n, scale_n = mod_vec(nxt, 0), mod_vec(nxt, 1)
        else:
            shift_n = scale_n = None
        x, h = _combine_ln(x, shared, ys, dest, wk, mod_vec(mod, 5), ln_gain[i, 1], ln_bias[i, 1], alpha,
                           nc, pitch, shift=shift_n, scale=scale_n)
    return x
```

```python
import functools
import math

import jax
import jax.numpy as jnp
import numpy as np
from jax import lax
from jax.experimental import pallas as pl
from jax.experimental.pallas import tpu as pltpu

HEAD_DIM = 128
N_MIXERS = 4
GQA_RATIO = 4
GRID_W = 64
ROPE_THETA = 10000.0
LN_EPS = 1e-5
RMS_EPS = 1e-6
N_MOD = 6
DIL_PATTERNS = ((128, 1), (512, 4), (2048, 16))
NAT_WIN_ROWS = 8
NAT_WIN_COLS = 16
N_EXPERT_GROUPS = 8
TOPK_GROUPS = 4
TOP_K = 8
ROUTED_SCALE = 2.5

VMEM_LIMIT_BYTES = 56 * 1024 * 1024
NEG_BIG = -1e30

F32 = jnp.float32
BF16 = jnp.bfloat16


def _cparams(*sem):
    return pltpu.CompilerParams(dimension_semantics=("arbitrary",) * len(sem), vmem_limit_bytes=VMEM_LIMIT_BYTES)


def _tile(n, pref):
    if n <= pref:
        return n
    while n % pref:
        pref //= 2
    return pref


def _dot(a, b):
    return jnp.dot(a, b, preferred_element_type=F32)


def _dot_nt(a, b):
    return lax.dot_general(a, b, (((1,), (1,)), ((), ())), preferred_element_type=F32)


def _matmul_kernel(a_ref, w_ref, o_ref, acc_ref, *, nk):
    k = pl.program_id(2)

    @pl.when(k == 0)
    def _():
        acc_ref[...] = jnp.zeros_like(acc_ref)

    acc_ref[...] += _dot(a_ref[...].astype(BF16), w_ref[...].astype(BF16))

    @pl.when(k == nk - 1)
    def _():
        o_ref[...] = acc_ref[...].astype(o_ref.dtype)


def _matmul(a, w, out_dtype, tm=2048, tn=1024, tk=1024, name="matmul"):
    m, kd = a.shape
    n = w.shape[1]
    tm, tn, tk = _tile(m, tm), _tile(n, tn), _tile(kd, tk)
    return pl.pallas_call(
        functools.partial(_matmul_kernel, nk=kd // tk),
        grid=(m // tm, n // tn, kd // tk),
        in_specs=[pl.BlockSpec((tm, tk), lambda i, j, k: (i, k)),
                  pl.BlockSpec((tk, tn), lambda i, j, k: (k, j))],
        out_specs=pl.BlockSpec((tm, tn), lambda i, j, k: (i, j)),
        out_shape=jax.ShapeDtypeStruct((m, n), out_dtype),
        scratch_shapes=[pltpu.VMEM((tm, tn), F32)],
        compiler_params=_cparams("parallel", "parallel", "arbitrary"),
        name=name,
    )(a, w)


def _ada_kernel(c_ref, w_ref, b_ref, o_ref, acc_ref, *, nk):
    k = pl.program_id(1)

    @pl.when(k == 0)
    def _():
        acc_ref[...] = jnp.zeros_like(acc_ref)

    c = c_ref[...]
    a = c * jax.nn.sigmoid(c)
    acc_ref[...] += _dot(a.astype(BF16), w_ref[...].astype(BF16))

    @pl.when(k == nk - 1)
    def _():
        o_ref[...] = acc_ref[...] + b_ref[...]


def _ada_proj(c_pad, w_ada, b_ada, tn=2048, tk=512):
    m, kd = c_pad.shape
    n = w_ada.shape[1]
    tn, tk = min(tn, n), min(tk, kd)
    return pl.pallas_call(
        functools.partial(_ada_kernel, nk=kd // tk),
        grid=(n // tn, kd // tk),
        in_specs=[pl.BlockSpec((m, tk), lambda j, k: (0, k)),
                  pl.BlockSpec((tk, tn), lambda j, k: (k, j)),
                  pl.BlockSpec((1, tn), lambda j, k: (0, j))],
        out_specs=pl.BlockSpec((m, tn), lambda j, k: (0, j)),
        out_shape=jax.ShapeDtypeStruct((m, n), F32),
        scratch_shapes=[pltpu.VMEM((m, tn), F32)],
        compiler_params=_cparams("parallel", "arbitrary"),
        name="ada_proj",
    )(c_pad, w_ada, b_ada.reshape(1, n))


def _modulate_kernel(x_ref, shift_ref, scale_ref, h_ref):
    h_ref[...] = (x_ref[...] * (1.0 + scale_ref[...]) + shift_ref[...]).astype(h_ref.dtype)


def _modulate(x, shift, scale, ts=512):
    b, s, d = x.shape
    ts = min(ts, s)
    row = pl.BlockSpec((None, ts, d), lambda bi, i: (bi, i, 0))
    vec = pl.BlockSpec((None, 1, d), lambda bi, i: (bi, 0, 0))
    return pl.pallas_call(
        _modulate_kernel,
        grid=(b, s // ts),
        in_specs=[row, vec, vec],
        out_specs=row,
        out_shape=jax.ShapeDtypeStruct((b, s, d), BF16),
        compiler_params=_cparams("parallel", "parallel"),
        name="modulate",
    )(x, shift, scale)


def _ln_kernel(x_ref, y_ref, gate_ref, g_ref, b_ref, shift_ref, scale_ref, xo_ref, h_ref, h3_ref, *, alpha, nc, pitch):
    ts = x_ref.shape[0]
    z = alpha * x_ref[...] + gate_ref[...] * y_ref[...].astype(F32)
    zc = z - jnp.mean(z, axis=-1, keepdims=True)
    var = jnp.mean(zc * zc, axis=-1, keepdims=True)
    xn = zc * lax.rsqrt(var + LN_EPS) * g_ref[...] + b_ref[...]
    xo_ref[...] = xn
    hf = xn * (1.0 + scale_ref[...]) + shift_ref[...]
    h_ref[...] = hf.astype(h_ref.dtype)
    _slab_store(h3_ref, hf, ts, nc, pitch)


def _ln_residual(x, y, gate, g, b, alpha, shift, scale, nc, pitch, ts=256):
    bsz, s, d = x.shape
    ts = _tile(s, ts)
    nsb = s // ts
    row = pl.BlockSpec((None, ts, d), lambda bi, i: (bi, i, 0))
    vec = pl.BlockSpec((None, 1, d), lambda bi, i: (bi, 0, 0))
    par = pl.BlockSpec((1, d), lambda bi, i: (0, 0))
    return pl.pallas_call(
        functools.partial(_ln_kernel, alpha=alpha, nc=nc, pitch=pitch),
        grid=(bsz, nsb),
        in_specs=[row, row, vec, par, par, vec, vec],
        out_specs=[row, row, pl.BlockSpec((ts * pitch, HEAD_DIM), lambda bi, i: (bi * nsb + i, 0))],
        out_shape=[jax.ShapeDtypeStruct((bsz, s, d), F32),
                   jax.ShapeDtypeStruct((bsz, s, d), BF16),
                   jax.ShapeDtypeStruct((bsz * s * pitch, HEAD_DIM), U32)],
        compiler_params=_cparams("arbitrary", "arbitrary"),
        name="ln_mixer",
    )(x, y, gate, g.reshape(1, d), b.reshape(1, d), shift, scale)


def _prep_kernel(x_ref, gain_ref, cos_ref, sin_ref, *o_refs, flags, half, dests):
    cos = cos_ref[...]
    sin = sin_ref[...]
    lane = lax.broadcasted_iota(jnp.int32, cos.shape, 1)
    first_half = (lane % (2 * half)) < half
    for j, flag in enumerate(flags):
        sl = slice(j * HEAD_DIM, (j + 1) * HEAD_DIM)
        o_ref = o_refs[dests[j][0]]
        osl = slice(dests[j][1] * HEAD_DIM, (dests[j][1] + 1) * HEAD_DIM)
        x = x_ref[:, sl].astype(F32)
        if flag & 1:
            x = x * lax.rsqrt(jnp.mean(x * x, axis=-1, keepdims=True) + RMS_EPS) * gain_ref[j:j + 1, :]
        if flag & 2:
            if 2 * half == HEAD_DIM:
                partner = pltpu.roll(x, half, 1)
            else:
                up = pltpu.roll(x, HEAD_DIM - half, 1)
                dn = pltpu.roll(x, half, 1)
                partner = jnp.where(first_half, up, dn)
            x = x * cos + partner * sin
        o_ref[:, osl] = x.astype(o_ref.dtype)


def _qk_prep(qkv, flags, gains, cos_t, sin_t, half, seq, groups=None, ts=256):
    t, n = qkv.shape
    nc = n // HEAD_DIM
    ts = _tile(seq, ts)
    nsb = seq // ts
    groups = [list(range(nc))] if groups is None else groups
    dests = {j: (gi, jj) for gi, grp in enumerate(groups) for jj, j in enumerate(grp)}
    outs = pl.pallas_call(
        functools.partial(_prep_kernel, flags=tuple(flags), half=half, dests=dests),
        grid=(t // ts,),
        in_specs=[pl.BlockSpec((ts, n), lambda i: (i, 0)),
                  pl.BlockSpec((nc, HEAD_DIM), lambda i: (0, 0)),
                  pl.BlockSpec((ts, HEAD_DIM), lambda i: (i % nsb, 0)),
                  pl.BlockSpec((ts, HEAD_DIM), lambda i: (i % nsb, 0))],
        out_specs=[pl.BlockSpec((ts, len(grp) * HEAD_DIM), lambda i: (i, 0)) for grp in groups],
        out_shape=[jax.ShapeDtypeStruct((t, len(grp) * HEAD_DIM), BF16) for grp in groups],
        compiler_params=_cparams("arbitrary"),
        name="qk_prep",
    )(qkv, gains, cos_t, sin_t)
    return outs[0] if len(groups) == 1 else outs


def _rope_tables(pos, dim):
    inv = ROPE_THETA ** (-jnp.arange(0, dim, 2, dtype=F32) / dim)
    ang = pos.astype(F32)[:, None] * inv[None, :]
    return jnp.cos(ang), jnp.sin(ang)


def _softmax_pv(s, v):
    m = jnp.max(s, axis=-1, keepdims=True)
    p = jnp.exp(s - m)
    l = jnp.sum(p, axis=-1, keepdims=True)
    return _dot(p.astype(BF16), v) / l


def _dense_head(q, k, v, scale):
    return _softmax_pv(_dot_nt(q, k) * scale, v)


def _gqa_attn_kernel(q_ref, k_ref, v_ref, o_ref, *, n_q, scale):
    k = k_ref[...]
    v = v_ref[...]
    for j in range(n_q):
        sl = slice(j * HEAD_DIM, (j + 1) * HEAD_DIM)
        o_ref[:, sl] = _dense_head(q_ref[:, sl], k, v, scale).astype(o_ref.dtype)


def _diff_attn_kernel(q_ref, k_ref, v_ref, lam_ref, gain_ref, o_ref, *, n_g, scale, lam_init):
    lp = lam_ref[...]
    lam = (jnp.exp(jnp.sum(lp[0:1] * lp[1:2], axis=-1, keepdims=True))
           - jnp.exp(jnp.sum(lp[2:3] * lp[3:4], axis=-1, keepdims=True)) + lam_init)
    v = v_ref[...]
    dv = v.shape[-1]
    for g in range(n_g):
        o = None
        for c in range(2):
            j = 2 * g + c
            q = q_ref[:, j * HEAD_DIM:(j + 1) * HEAD_DIM]
            oc = _dense_head(q, k_ref[:, c * HEAD_DIM:(c + 1) * HEAD_DIM], v, scale)
            o = oc if c == 0 else o - lam * oc
        o = o * lax.rsqrt(jnp.mean(o * o, axis=-1, keepdims=True) + RMS_EPS) * gain_ref[...]
        o_ref[:, g * dv:(g + 1) * dv] = (o * (1.0 - lam_init)).astype(o_ref.dtype)


def _gqa_attention(qkv, bsz, seq, n_kv, tq=256):
    g = GQA_RATIO
    n_h = n_kv * g
    tq = min(tq, seq)
    nq = seq // tq
    qw = g * HEAD_DIM
    return pl.pallas_call(
        functools.partial(_gqa_attn_kernel, n_q=g, scale=HEAD_DIM ** -0.5),
        grid=(bsz, n_kv, nq),
        in_specs=[pl.BlockSpec((tq, qw), lambda b, h, i: (b * nq + i, h)),
                  pl.BlockSpec((seq, HEAD_DIM), lambda b, h, i: (b, n_h + h)),
                  pl.BlockSpec((seq, HEAD_DIM), lambda b, h, i: (b, n_h + n_kv + h))],
        out_specs=pl.BlockSpec((tq, qw), lambda b, h, i: (b * nq + i, h)),
        out_shape=jax.ShapeDtypeStruct((bsz * seq, n_h * HEAD_DIM), BF16),
        compiler_params=_cparams("parallel", "parallel", "arbitrary"),
        name="gqa_attention",
    )(qkv, qkv, qkv)


def _diff_attention(qkv, lam_params, sub_gain, lam_init, bsz, seq, n_kv, tq=256):
    g = GQA_RATIO
    n_h = n_kv * g
    tq = min(tq, seq)
    nq = seq // tq
    qw = g * 2 * HEAD_DIM
    dv = 2 * HEAD_DIM
    k_blk0 = (2 * n_h * HEAD_DIM) // dv
    v_blk0 = k_blk0 + n_kv
    return pl.pallas_call(
        functools.partial(_diff_attn_kernel, n_g=g, scale=HEAD_DIM ** -0.5, lam_init=lam_init),
        grid=(bsz, n_kv, nq),
        in_specs=[pl.BlockSpec((tq, qw), lambda b, h, i: (b * nq + i, h)),
                  pl.BlockSpec((seq, dv), lambda b, h, i: (b, k_blk0 + h)),
                  pl.BlockSpec((seq, dv), lambda b, h, i: (b, v_blk0 + h)),
                  pl.BlockSpec((4, HEAD_DIM), lambda b, h, i: (0, 0)),
                  pl.BlockSpec((1, dv), lambda b, h, i: (0, 0))],
        out_specs=pl.BlockSpec((tq, g * dv), lambda b, h, i: (b * nq + i, h)),
        out_shape=jax.ShapeDtypeStruct((bsz * seq, n_h * dv), BF16),
        compiler_params=_cparams("parallel", "parallel", "arbitrary"),
        name="diff_attention",
    )(qkv, qkv, qkv, lam_params.astype(F32), sub_gain.reshape(1, dv).astype(F32))


def _dil_attn_kernel(q_ref, k_ref, v_ref, o_ref, st_ref, *, n_g, half, sub, scale):
    tl = q_ref.shape[0]
    length = k_ref.shape[0]
    win = min(length, sub + 2 * half)
    i = pl.program_id(3)
    lane = lax.broadcasted_iota(jnp.int32, (sub, HEAD_DIM), 1)
    for sb in range(tl // sub):
        qs = i * tl + sb * sub
        start = jnp.clip(qs - half, 0, length - win)
        start = pl.multiple_of(start, half)
        kw = k_ref[pl.ds(start, win), :]
        vw = v_ref[pl.ds(start, win), :]
        kpos = start + lax.broadcasted_iota(jnp.int32, (sub, win), 1)
        qpos = qs + lax.broadcasted_iota(jnp.int32, (sub, win), 0)
        valid = jnp.abs(kpos - qpos) <= half
        stats = jnp.zeros((sub, HEAD_DIM), F32)
        for g in range(n_g):
            sl = slice(g * HEAD_DIM, (g + 1) * HEAD_DIM)
            s = _dot_nt(q_ref[sb * sub:(sb + 1) * sub, sl], kw) * scale
            s = jnp.where(valid, s, NEG_BIG)
            m = jnp.max(s, axis=-1, keepdims=True)
            p = jnp.exp(s - m)
            l = jnp.sum(p, axis=-1, keepdims=True)
            o_ref[sb * sub:(sb + 1) * sub, sl] = (_dot(p.astype(BF16), vw) / l).astype(o_ref.dtype)
            stats = jnp.where(lane == g, m, stats)
            stats = jnp.where(lane == n_g + g, l, stats)
        st_ref[sb * sub:(sb + 1) * sub, :] = stats


def _dilated_group(qkv, window, dilation, bsz, seq, n_kv, tl=512, sub=128):
    g = GQA_RATIO
    n_h = n_kv * g
    d = dilation
    length = seq // d
    half = window // (2 * d)
    sub = min(sub, length)
    tl = min(tl, length)
    nl = length // tl
    n_cols = qkv.shape[1]
    qkv_v = qkv.reshape(bsz * length, d * n_cols)
    qw = g * HEAD_DIM
    assert n_cols % qw == 0 and length % tl == 0 and tl % sub == 0
    q_blk = lambda b, r, h, i: (b * nl + i, (r * n_cols) // qw + h)
    k_col0 = n_h * HEAD_DIM
    v_col0 = (n_h + n_kv) * HEAD_DIM
    k_blk = lambda b, r, h, i: (b, (r * n_cols + k_col0) // HEAD_DIM + h)
    v_blk = lambda b, r, h, i: (b, (r * n_cols + v_col0) // HEAD_DIM + h)
    o, st = pl.pallas_call(
        functools.partial(_dil_attn_kernel, n_g=g, half=half, sub=sub, scale=HEAD_DIM ** -0.5),
        grid=(bsz, d, n_kv, nl),
        in_specs=[pl.BlockSpec((tl, qw), q_blk),
                  pl.BlockSpec((length, HEAD_DIM), k_blk),
                  pl.BlockSpec((length, HEAD_DIM), v_blk)],
        out_specs=[pl.BlockSpec((tl, qw), lambda b, r, h, i: (b * nl + i, r * n_kv + h)),
                   pl.BlockSpec((tl, HEAD_DIM), lambda b, r, h, i: (b * nl + i, r * n_kv + h))],
        out_shape=[jax.ShapeDtypeStruct((bsz * length, d * n_h * HEAD_DIM), F32),
                   jax.ShapeDtypeStruct((bsz * length, d * n_kv * HEAD_DIM), F32)],
        compiler_params=_cparams("parallel", "parallel", "parallel", "arbitrary"),
        name=f"dilated_attention_d{d}",
    )(qkv_v, qkv_v, qkv_v)
    return o.reshape(bsz * seq, n_h * HEAD_DIM), st.reshape(bsz * seq, n_kv * HEAD_DIM)


def _dil_merge_kernel(*refs, n_grp, n_kv, n_g):
    o_refs, st_refs, out_ref = refs[:n_grp], refs[n_grp:2 * n_grp], refs[2 * n_grp]
    for kv in range(n_kv):
        st = [r[:, kv * HEAD_DIM:(kv + 1) * HEAD_DIM] for r in st_refs]
        for g in range(n_g):
            h = kv * n_g + g
            ms = [s[:, g:g + 1] for s in st]
            ls = [s[:, n_g + g:n_g + g + 1] for s in st]
            mmax = functools.reduce(jnp.maximum, ms)
            ws = [l * jnp.exp(m - mmax) for m, l in zip(ms, ls)]
            wsum = functools.reduce(lambda a, b: a + b, ws)
            sl = slice(h * HEAD_DIM, (h + 1) * HEAD_DIM)
            acc = None
            for w, o_ref in zip(ws, o_refs):
                term = (w / wsum) * o_ref[:, sl]
                acc = term if acc is None else acc + term
            out_ref[:, sl] = acc.astype(out_ref.dtype)


def _dilated_merge(outs, stats, n_kv, ts=256):
    t, n = outs[0].shape
    ts = min(ts, t)
    n_grp = len(outs)
    o_spec = pl.BlockSpec((ts, n), lambda i: (i, 0))
    s_spec = pl.BlockSpec((ts, stats[0].shape[1]), lambda i: (i, 0))
    return pl.pallas_call(
        functools.partial(_dil_merge_kernel, n_grp=n_grp, n_kv=n_kv, n_g=GQA_RATIO),
        grid=(t // ts,),
        in_specs=[o_spec] * n_grp + [s_spec] * n_grp,
        out_specs=o_spec,
        out_shape=jax.ShapeDtypeStruct((t, n), BF16),
        compiler_params=_cparams("parallel"),
        name="dilated_merge",
    )(*outs, *stats)


def _nat_attn_kernel(q_ref, k_ref, v_ref, bias_ref, o_ref, *, n_g, rows, scale):
    rows_per_step = q_ref.shape[0] // GRID_W
    i = pl.program_id(2)
    nk = NAT_WIN_ROWS * GRID_W
    for rr in range(rows_per_step):
        r = i * rows_per_step + rr
        rs = jnp.clip(r - NAT_WIN_ROWS // 2, 0, rows - NAT_WIN_ROWS)
        ro = r - rs
        start = pl.multiple_of(rs * GRID_W, GRID_W)
        kw = k_ref[pl.ds(start, nk), :]
        vw = v_ref[pl.ds(start, nk), :]
        rsl = slice(rr * GRID_W, (rr + 1) * GRID_W)
        q = jnp.concatenate([q_ref[rsl, g * HEAD_DIM:(g + 1) * HEAD_DIM] for g in range(n_g)], axis=0)
        s = _dot_nt(q, kw) * scale + bias_ref[ro]
        o = _softmax_pv(s, vw)
        for g in range(n_g):
            o_ref[rsl, g * HEAD_DIM:(g + 1) * HEAD_DIM] = o[g * GRID_W:(g + 1) * GRID_W].astype(o_ref.dtype)


def _nat_bias_table(rpb, n_kv):
    n_h = rpb.shape[0]
    g = n_h // n_kv
    nr, ncol, w = NAT_WIN_ROWS, NAT_WIN_COLS, GRID_W
    ro, ki = np.meshgrid(np.arange(nr), np.arange(nr), indexing="ij")
    rowsel = np.zeros((nr, nr, 2 * nr - 1), np.float32)
    rowsel[ro, ki, ki - ro + nr - 1] = 1.0
    c, kc = np.meshgrid(np.arange(w), np.arange(w), indexing="ij")
    cs = np.clip(c - ncol // 2, 0, w - ncol)
    valid = (kc >= cs) & (kc < cs + ncol)
    colsel = np.zeros((2 * ncol - 1, w, w), np.float32)
    colsel[(kc - c + ncol - 1)[valid], c[valid], kc[valid]] = 1.0
    rows = jnp.einsum("hij,rki->hrkj", rpb.astype(F32), rowsel, precision=lax.Precision.HIGHEST)
    tab = jnp.einsum("hrkj,jcq->hrckq", rows, colsel, precision=lax.Precision.HIGHEST)
    tab = jnp.where(valid[None, None, :, None, :], tab, NEG_BIG)
    tab = tab.reshape(n_kv, g, nr, w, nr * w)
    return tab.transpose(0, 2, 1, 3, 4).reshape(n_kv, nr, g * w, nr * w)


def _nat_attention(qkv, bias_tab, bsz, seq, n_kv, rows_per_step=8):
    g = GQA_RATIO
    n_h = n_kv * g
    rows = seq // GRID_W
    assert rows >= NAT_WIN_ROWS
    rows_per_step = min(rows_per_step, rows)
    tq = rows_per_step * GRID_W
    nq = seq // tq
    qw = g * HEAD_DIM
    return pl.pallas_call(
        functools.partial(_nat_attn_kernel, n_g=g, rows=rows, scale=HEAD_DIM ** -0.5),
        grid=(bsz, n_kv, nq),
        in_specs=[pl.BlockSpec((tq, qw), lambda b, h, i: (b * nq + i, h)),
                  pl.BlockSpec((seq, HEAD_DIM), lambda b, h, i: (b, n_h + h)),
                  pl.BlockSpec((seq, HEAD_DIM), lambda b, h, i: (b, n_h + n_kv + h)),
                  pl.BlockSpec((None,) + bias_tab.shape[1:], lambda b, h, i: (h, 0, 0, 0))],
        out_specs=pl.BlockSpec((tq, qw), lambda b, h, i: (b * nq + i, h)),
        out_shape=jax.ShapeDtypeStruct((bsz * seq, n_h * HEAD_DIM), BF16),
        compiler_params=_cparams("parallel", "parallel", "arbitrary"),
        name="nat_attention",
    )(qkv, qkv, qkv, bias_tab)


def _router_kernel(h_ref, w_ref, bias_ref, e_ref, wt_ref, rank_ref, cnt_ref, carry_ref, *, n_e):
    i = pl.program_id(0)
    tm = h_ref.shape[0]
    per_group = n_e // N_EXPERT_GROUPS

    @pl.when(i == 0)
    def _():
        carry_ref[...] = jnp.zeros_like(carry_ref)

    h = h_ref[...]
    w = w_ref[...]
    w_hi = w.astype(BF16)
    w_lo = (w - w_hi.astype(F32)).astype(BF16)
    logits = _dot_nt(w_hi, h) + _dot_nt(w_lo, h)
    scores = jax.nn.sigmoid(logits)
    biased = scores + bias_ref[...]

    b3 = biased.reshape(N_EXPERT_GROUPS, per_group, tm)
    idx3 = lax.broadcasted_iota(jnp.int32, b3.shape, 1)
    m1 = jnp.max(b3, axis=1, keepdims=True)
    i1 = jnp.min(jnp.where(b3 == m1, idx3, per_group), axis=1, keepdims=True)
    m2 = jnp.max(jnp.where(idx3 == i1, -jnp.inf, b3), axis=1, keepdims=True)
    gs = (m1 + m2).reshape(N_EXPERT_GROUPS, tm)

    gidx = lax.broadcasted_iota(jnp.int32, gs.shape, 0)
    gsel = jnp.zeros(gs.shape, jnp.bool_)
    for _ in range(TOPK_GROUPS):
        gm = jnp.max(gs, axis=0, keepdims=True)
        gi = jnp.min(jnp.where(gs == gm, gidx, N_EXPERT_GROUPS), axis=0, keepdims=True)
        hit = gidx == gi
        gsel = gsel | hit
        gs = jnp.where(hit, -jnp.inf, gs)
    emask = jnp.broadcast_to(gsel.reshape(N_EXPERT_GROUPS, 1, tm), b3.shape).reshape(n_e, tm)
    masked = jnp.where(emask, biased, -jnp.inf)

    eidx = lax.broadcasted_iota(jnp.int32, masked.shape, 0)
    sel = jnp.zeros(masked.shape, jnp.bool_)
    top_e, top_w, hits = [], [], []
    for _ in range(TOP_K):
        mx = jnp.max(masked, axis=0, keepdims=True)
        ei = jnp.min(jnp.where(masked == mx, eidx, n_e), axis=0, keepdims=True)
        hit = eidx == ei
        top_e.append(ei)
        top_w.append(jnp.sum(jnp.where(hit, scores, 0.0), axis=0, keepdims=True))
        hits.append(hit)
        sel = sel | hit
        masked = jnp.where(hit, -jnp.inf, masked)
    wsum = functools.reduce(lambda a, b: a + b, top_w)

    selb = jnp.where(sel, 1.0, 0.0).astype(BF16)
    tri = (lax.broadcasted_iota(jnp.int32, (tm, tm), 0) < lax.broadcasted_iota(jnp.int32, (tm, tm), 1))
    ranks = _dot(selb, jnp.where(tri, 1.0, 0.0).astype(BF16)) + carry_ref[:, 0:1]
    carry_ref[...] = carry_ref[...] + jnp.sum(selb.astype(F32), axis=1, keepdims=True)

    for k in range(TOP_K):
        e_ref[k:k + 1, :] = top_e[k]
        wt_ref[k:k + 1, :] = top_w[k] / wsum * ROUTED_SCALE
        rank_ref[k:k + 1, :] = jnp.sum(jnp.where(hits[k], ranks, 0.0), axis=0, keepdims=True).astype(jnp.int32)
    cnt_ref[...] = carry_ref[...]


def _router(x, router_w_t, router_bias, tm=512):
    t, d = x.shape
    n_e = router_w_t.shape[0]
    tm = min(tm, t)
    tok = pl.BlockSpec((TOP_K, tm), lambda i: (0, i))
    e, w, rank, cnt = pl.pallas_call(
        functools.partial(_router_kernel, n_e=n_e),
        grid=(t // tm,),
        in_specs=[pl.BlockSpec((tm, d), lambda i: (i, 0)),
                  pl.BlockSpec((n_e, d), lambda i: (0, 0)),
                  pl.BlockSpec((n_e, 1), lambda i: (0, 0))],
        out_specs=[tok, tok, tok, pl.BlockSpec((n_e, HEAD_DIM), lambda i: (0, 0))],
        out_shape=[jax.ShapeDtypeStruct((TOP_K, t), jnp.int32),
                   jax.ShapeDtypeStruct((TOP_K, t), F32),
                   jax.ShapeDtypeStruct((TOP_K, t), jnp.int32),
                   jax.ShapeDtypeStruct((n_e, HEAD_DIM), F32)],
        scratch_shapes=[pltpu.VMEM((n_e, HEAD_DIM), F32)],
        compiler_params=_cparams("arbitrary"),
        name="moe_router",
    )(x, router_w_t, router_bias.reshape(n_e, 1).astype(F32))
    return e, w, rank, cnt[:, 0]


U32 = jnp.uint32
HI_MASK = 0xFFFF0000


def _slab_pitch(nc):
    pitch = nc + 8
    return pitch if (pitch // 8) % 2 else pitch + 8


def _bf16_bits(x):
    return lax.bitcast_convert_type(x.astype(BF16).astype(F32), U32)


def _slab_words(ref, c, n_tok, pitch, row0=0):
    w = ref[pl.ds(row0 + c, n_tok, stride=pitch), :]
    lo = lax.bitcast_convert_type(w << 16, F32)
    hi = lax.bitcast_convert_type(w & U32(HI_MASK), F32)
    return lo, hi


def _slab_load(ref, n_tok, nc, pitch):
    halves = [_slab_words(ref, c, n_tok, pitch) for c in range(nc)]
    return jnp.concatenate([lo.astype(BF16) for lo, _ in halves] + [hi.astype(BF16) for _, hi in halves], axis=1)


def _slab_store(ref, val, n_tok, nc, pitch):
    half = nc * HEAD_DIM
    for c in range(nc):
        lo = _bf16_bits(val[:, c * HEAD_DIM:(c + 1) * HEAD_DIM]) >> 16
        hi = _bf16_bits(val[:, half + c * HEAD_DIM:half + (c + 1) * HEAD_DIM]) & U32(HI_MASK)
        ref[pl.ds(c, n_tok, stride=pitch), :] = lo | hi
    for r in range(nc, pitch):
        ref[pl.ds(r, n_tok, stride=pitch), :] = jnp.zeros((n_tok, HEAD_DIM), U32)


def _gather_pipeline(i, n_act, idx_hbm, src_hbm, idx_smem, buf, idx_sem, g_sem, *, n_idx, nc, pitch):
    slot = i % 2
    rows = n_idx * nc

    def idx_copy(tile, sl):
        return pltpu.make_async_copy(idx_hbm.at[pl.ds(tile * n_idx, n_idx)],
                                     idx_smem.at[pl.ds(sl * n_idx, n_idx)], idx_sem.at[sl])

    def issue(sl):
        def body(jj, carry):
            for prio in range(2):
                j = 2 * jj + prio
                row0 = idx_smem[sl * n_idx + j]
                pltpu.make_async_copy(src_hbm.at[pl.ds(pl.multiple_of(row0, 8), nc), :],
                                      buf.at[sl, pl.ds(pl.multiple_of(j * pitch, 8), nc), :],
                                      g_sem.at[sl]).start(priority=prio)
            return carry
        lax.fori_loop(0, n_idx // 2, body, 0, unroll=8)

    @pl.when(i == 0)
    def _():
        idx_copy(0, 0).start()
        idx_copy(0, 0).wait()
        issue(0)

        @pl.when(1 < n_act)
        def _():
            idx_copy(1, 1).start()

    @pl.when(i + 1 < n_act)
    def _():
        idx_copy(i + 1, 1 - slot).wait()
        issue(1 - slot)

    @pl.when(i + 2 < n_act)
    def _():
        idx_copy(i + 2, slot).start()

    pltpu.make_async_copy(src_hbm.at[pl.ds(0, rows), :], buf.at[slot, pl.ds(0, rows), :], g_sem.at[slot]).wait()


def _experts_kernel(te_ref, nt_ref, idx_hbm, h3_hbm, wg_ref, wu_ref, wd_ref, ys_ref,
                    idx_smem, xbuf, idx_sem, g_sem, *, tm, nc, pitch):
    i = pl.program_id(0)
    nt = nt_ref[0]

    @pl.when(i < nt)
    def _():
        _gather_pipeline(i, nt, idx_hbm, h3_hbm, idx_smem, xbuf, idx_sem, g_sem, n_idx=tm, nc=nc, pitch=pitch)
        x = _slab_load(xbuf.at[i % 2], tm, nc, pitch)
        a = _dot(x, wg_ref[...])
        u = _dot(x, wu_ref[...])
        hmid = (a * jax.nn.sigmoid(a) * u).astype(BF16)
        y = _dot(hmid, wd_ref[...])
        _slab_store(ys_ref, y, tm, nc, pitch)

    @pl.when(i >= nt)
    def _():
        ys_ref[...] = jnp.zeros_like(ys_ref)


def _routed_experts(h3, src_tok, tile_expert, n_tiles, w_gate, w_up, w_down, layer, tm, nc, pitch):
    n_tiles_max = tile_expert.shape[0]
    d = w_gate.shape[-2]
    hid = w_gate.shape[-1]
    return pl.pallas_call(
        functools.partial(_experts_kernel, tm=tm, nc=nc, pitch=pitch),
        grid_spec=pltpu.PrefetchScalarGridSpec(
            num_scalar_prefetch=2,
            grid=(n_tiles_max,),
            in_specs=[pl.BlockSpec(memory_space=pl.ANY),
                      pl.BlockSpec(memory_space=pl.ANY),
                      pl.BlockSpec((None, None, d, hid), lambda i, te, nt: (layer, te[i], 0, 0)),
                      pl.BlockSpec((None, None, d, hid), lambda i, te, nt: (layer, te[i], 0, 0)),
                      pl.BlockSpec((None, None, hid, d), lambda i, te, nt: (layer, te[i], 0, 0))],
            out_specs=pl.BlockSpec((tm * pitch, HEAD_DIM), lambda i, te, nt: (i, 0)),
            scratch_shapes=[pltpu.SMEM((2 * tm,), jnp.int32),
                            pltpu.VMEM((2, tm * pitch, HEAD_DIM), U32),
                            pltpu.SemaphoreType.DMA((2,)),
                            pltpu.SemaphoreType.DMA((2,))],
        ),
        out_shape=jax.ShapeDtypeStruct((n_tiles_max * tm * pitch, HEAD_DIM), U32),
        compiler_params=_cparams("arbitrary"),
        name="moe_experts",
    )(tile_expert, n_tiles, src_tok, h3, w_gate, w_up, w_down)


def _ffn_kernel(x_ref, wg_ref, wu_ref, wd_ref, o_ref):
    x = x_ref[...]
    a = _dot(x, wg_ref[...])
    u = _dot(x, wu_ref[...])
    hmid = (a * jax.nn.sigmoid(a) * u).astype(BF16)
    o_ref[...] = _dot(hmid, wd_ref[...]).astype(o_ref.dtype)


def _shared_ffn(h, w_gate, w_up, w_down, layer, tm=1024):
    t, d = h.shape
    hid = w_gate.shape[-1]
    tm = _tile(t, tm)
    return pl.pallas_call(
        _ffn_kernel,
        grid=(t // tm,),
        in_specs=[pl.BlockSpec((tm, d), lambda i: (i, 0)),
                  pl.BlockSpec((None, d, hid), lambda i: (layer, 0, 0)),
                  pl.BlockSpec((None, d, hid), lambda i: (layer, 0, 0)),
                  pl.BlockSpec((None, hid, d), lambda i: (layer, 0, 0))],
        out_specs=pl.BlockSpec((tm, d), lambda i: (i, 0)),
        out_shape=jax.ShapeDtypeStruct((t, d), BF16),
        compiler_params=_cparams("arbitrary"),
        name="moe_shared",
    )(h, w_gate, w_up, w_down)


def _combine_ln_kernel(*refs, alpha, ts, nc, pitch, n_steps, with_h):
    it = iter(refs)
    idx_hbm, ys_hbm, x_ref, sh_ref, wk_ref, gate_ref, g_ref, b_ref = [next(it) for _ in range(8)]
    if with_h:
        shift_ref, scale_ref = next(it), next(it)
    xo_ref = next(it)
    if with_h:
        h_ref = next(it)
    idx_smem, buf, idx_sem, g_sem = [next(it) for _ in range(4)]

    i = pl.program_id(0)
    _gather_pipeline(i, n_steps, idx_hbm, ys_hbm, idx_smem, buf, idx_sem, g_sem,
                     n_idx=TOP_K * ts, nc=nc, pitch=pitch)
    slabs = buf.at[i % 2]
    wk = wk_ref[...]
    wkb = [jnp.broadcast_to(wk[:, k:k + 1], (ts, HEAD_DIM)) for k in range(TOP_K)]
    lo_pieces, hi_pieces = [], []
    for c in range(nc):
        acc_lo = acc_hi = None
        for k in range(TOP_K):
            lo, hi = _slab_words(slabs, c, ts, pitch, row0=k * ts * pitch)
            acc_lo = wkb[k] * lo if acc_lo is None else acc_lo + wkb[k] * lo
            acc_hi = wkb[k] * hi if acc_hi is None else acc_hi + wkb[k] * hi
        lo_pieces.append(acc_lo)
        hi_pieces.append(acc_hi)
    y = sh_ref[...].astype(F32) + jnp.concatenate(lo_pieces + hi_pieces, axis=1)
    z = alpha * x_ref[...] + gate_ref[...] * y
    zc = z - jnp.mean(z, axis=-1, keepdims=True)
    var = jnp.mean(zc * zc, axis=-1, keepdims=True)
    xn = zc * lax.rsqrt(var + LN_EPS) * g_ref[...] + b_ref[...]
    xo_ref[...] = xn
    if with_h:
        h_ref[...] = (xn * (1.0 + scale_ref[...]) + shift_ref[...]).astype(h_ref.dtype)


def _combine_ln(x, shared, ys, dest, wk, gate, g, b, alpha, nc, pitch, shift=None, scale=None, ts=128):
    bsz, s, d = x.shape
    t = bsz * s
    ts = _tile(s, ts)
    nsb = s // ts
    n_steps = t // ts
    with_h = shift is not None
    idx = (dest * pitch).reshape(TOP_K, n_steps, ts).transpose(1, 0, 2).reshape(-1)
    row = pl.BlockSpec((ts, d), lambda i: (i, 0))
    vec = pl.BlockSpec((None, 1, d), lambda i: (i // nsb, 0, 0))
    par = pl.BlockSpec((1, d), lambda i: (0, 0))
    hbm = pl.BlockSpec(memory_space=pl.ANY)
    args = [idx, ys, x.reshape(t, d), shared, wk, gate, g.reshape(1, d), b.reshape(1, d)]
    specs = [hbm, hbm, row, row, pl.BlockSpec((ts, TOP_K), lambda i: (i, 0)), vec, par, par]
    if with_h:
        args += [shift, scale]
        specs += [vec, vec]
    out_shape = [jax.ShapeDtypeStruct((t, d), F32)]
    out_specs = [row]
    if with_h:
        out_shape.append(jax.ShapeDtypeStruct((t, d), BF16))
        out_specs.append(row)
    outs = pl.pallas_call(
        functools.partial(_combine_ln_kernel, alpha=alpha, ts=ts, nc=nc, pitch=pitch, n_steps=n_steps, with_h=with_h),
        grid=(n_steps,),
        in_specs=specs,
        out_specs=out_specs,
        out_shape=out_shape,
        scratch_shapes=[pltpu.SMEM((2 * TOP_K * ts,), jnp.int32),
                        pltpu.VMEM((2, TOP_K * ts * pitch, HEAD_DIM), U32),
                        pltpu.SemaphoreType.DMA((2,)),
                        pltpu.SemaphoreType.DMA((2,))],
        compiler_params=_cparams("arbitrary"),
        name="moe_combine_ln",
    )(*args)
    x_new = outs[0].reshape(bsz, s, d)
    return (x_new, outs[1].reshape(bsz, s, d)) if with_h else (x_new, None)


def _moe(h, h3, router_w, router_bias, w_gate, w_up, w_down, s_gate, s_up, s_down, layer, nc, pitch, tm=256):
    t, d = h.shape
    n_e = router_w.shape[1]
    tm = min(tm, t)
    top_e, top_w, rank, counts = _router(h, router_w.T.astype(F32), router_bias)

    counts = counts.astype(jnp.int32)
    padded = ((counts + tm - 1) // tm) * tm
    ends = jnp.cumsum(padded)
    offsets = ends - padded
    n_tiles_max = (t * TOP_K + n_e * (tm - 1)) // tm
    n_tiles = (ends[-1] // tm).astype(jnp.int32).reshape(1)
    tile_start = jnp.arange(n_tiles_max, dtype=jnp.int32) * tm
    tile_expert = jnp.minimum(jnp.sum((ends[None, :] <= tile_start[:, None]).astype(jnp.int32), axis=1), n_e - 1)
    onehot = top_e[:, :, None] == jnp.arange(n_e, dtype=jnp.int32)[None, None, :]
    dest = rank + jnp.sum(jnp.where(onehot, offsets[None, None, :], 0), axis=-1)
    tok_row0 = jnp.broadcast_to(jnp.arange(t, dtype=jnp.int32)[None, :] * pitch, dest.shape)
    src_row0 = jnp.zeros((n_tiles_max * tm,), jnp.int32).at[dest.reshape(-1)].set(
        tok_row0.reshape(-1), unique_indices=True)

    ys = _routed_experts(h3, src_row0, tile_expert, n_tiles, w_gate, w_up, w_down, layer, tm, nc, pitch)
    shared = _shared_ffn(h, s_gate, s_up, s_down, layer)
    return shared, ys, dest, top_w.T


def kernel(x, c, w_ada, b_ada, ada_table, ln_gain, ln_bias, gqa_wqkv, gqa_wo, gqa_q_gain, gqa_k_gain, dil_wqkv, dil_wo, nat_wqkv, nat_wo, nat_rpb, dif_wqkv, dif_wo, dif_lambda, dif_subln_gain, router_w, router_bias, exp_w_gate, exp_w_up, exp_w_down, sh_w_gate, sh_w_up, sh_w_down):
    bsz, seq, d = x.shape
    depth = ada_table.shape[0]
    t = bsz * seq
    alpha = (2 * depth) ** 0.25
    n_h = d // HEAD_DIM
    n_kv = n_h // GQA_RATIO
    n_h2 = d // (2 * HEAD_DIM)
    n_kv2 = n_h2 // GQA_RATIO
    n_grp = len(DIL_PATTERNS)
    nc = d // (2 * HEAD_DIM)
    pitch = _slab_pitch(nc)

    pos = jnp.arange(seq)
    cos1, sin1 = _rope_tables(pos, HEAD_DIM)
    cos1_t = jnp.concatenate([cos1, cos1], axis=-1)
    sin1_t = jnp.concatenate([-sin1, sin1], axis=-1)
    cr, sr = _rope_tables(pos // GRID_W, HEAD_DIM // 2)
    cc, sc = _rope_tables(pos % GRID_W, HEAD_DIM // 2)
    cos2_t = jnp.concatenate([cr, cr, cc, cc], axis=-1)
    sin2_t = jnp.concatenate([-sr, sr, -sc, sc], axis=-1)

    pad_rows = 16
    c_pad = jnp.zeros((pad_rows, d), F32).at[:bsz].set(c.astype(F32))
    mod_shared = _ada_proj(c_pad, w_ada, b_ada)[:bsz].reshape(bsz, N_MOD, d)

    def mod_vec(mod, j):
        return mod[:, j:j + 1, :]

    expert_w = [w.astype(BF16) for w in (exp_w_gate, exp_w_up, exp_w_down)]
    shared_w = [w.astype(BF16) for w in (sh_w_gate, sh_w_up, sh_w_down)]

    mod = mod_shared + ada_table[0]
    h = _modulate(x, mod_vec(mod, 0), mod_vec(mod, 1))
    for i in range(depth):
        kind, j = i % N_MIXERS, i // N_MIXERS
        mod = mod_shared + ada_table[i]
        h2 = h.reshape(t, d)
        if kind == 0:
            qkv = _matmul(h2, gqa_wqkv[j], F32, name="gqa_qkv")
            flags = [3] * (n_h + n_kv) + [0] * n_kv
            gains = jnp.concatenate([jnp.broadcast_to(gqa_q_gain[j], (n_h, HEAD_DIM)),
                                     jnp.broadcast_to(gqa_k_gain[j], (n_kv, HEAD_DIM)),
                                     jnp.ones((n_kv, HEAD_DIM), F32)]).astype(F32)
            qkv = _qk_prep(qkv, flags, gains, cos2_t, sin2_t, HEAD_DIM // 4, seq)
            o = _gqa_attention(qkv, bsz, seq, n_kv)
            wo = gqa_wo[j]
        elif kind == 1:
            qkv = _matmul(h2, dil_wqkv[j], F32, name="dil_qkv")
            n_qk = n_grp * (n_h2 + n_kv2)
            flags = [2] * n_qk + [0] * (n_grp * n_kv2)
            gains = jnp.ones((n_qk + n_grp * n_kv2, HEAD_DIM), F32)
            k0, v0 = n_grp * n_h2, n_grp * (n_h2 + n_kv2)
            groups = [list(range(grp * n_h2, (grp + 1) * n_h2))
                      + list(range(k0 + grp * n_kv2, k0 + (grp + 1) * n_kv2))
                      + list(range(v0 + grp * n_kv2, v0 + (grp + 1) * n_kv2)) for grp in range(n_grp)]
            qkv_g = _qk_prep(qkv, flags, gains, cos1_t, sin1_t, HEAD_DIM // 2, seq, groups=groups)
            outs, stats = [], []
            for grp, (window, dilation) in enumerate(DIL_PATTERNS):
                og, sg = _dilated_group(qkv_g[grp], window, dilation, bsz, seq, n_kv2)
                outs.append(og)
                stats.append(sg)
            o = _dilated_merge(outs, stats, n_kv2)
            wo = dil_wo[j]
        elif kind == 2:
            qkv = _matmul(h2, nat_wqkv[j], BF16, name="nat_qkv")
            o = _nat_attention(qkv, _nat_bias_table(nat_rpb[j], n_kv), bsz, seq, n_kv)
            wo = nat_wo[j]
        else:
            qkv = _matmul(h2, dif_wqkv[j], F32, name="dif_qkv")
            n_qk = 2 * n_h2 + 2 * n_kv2
            flags = [2] * n_qk + [0] * (2 * n_kv2)
            gains = jnp.ones((n_qk + 2 * n_kv2, HEAD_DIM), F32)
            qkv = _qk_prep(qkv, flags, gains, cos1_t, sin1_t, HEAD_DIM // 2, seq)
            lam_init = 0.8 - 0.6 * math.exp(-0.3 * i)
            o = _diff_attention(qkv, dif_lambda[j], dif_subln_gain[j], lam_init, bsz, seq, n_kv2)
            wo = dif_wo[j]
        y = _matmul(o, wo, BF16, name="mixer_out").reshape(bsz, seq, d)
        x, h, h3 = _ln_residual(x, y, mod_vec(mod, 2), ln_gain[i, 0], ln_bias[i, 0], alpha,
                                mod_vec(mod, 3), mod_vec(mod, 4), nc, pitch)

        shared, ys, dest, wk = _moe(h.reshape(t, d), h3, router_w[i], router_bias[i],
                                    *expert_w, *shared_w, i, nc, pitch)
        if i + 1 < depth:
            nxt = mod_shared + ada_table[i + 1]
            shift_n, scale_n = mod_vec(nxt, 0), mod_vec(nxt, 1)
        else:
            shift_n = scale_n = None
        x, h = _combine_ln(x, shared, ys, dest, wk, mod_vec(mod, 5), ln_gain[i, 1], ln_bias[i, 1], alpha,
                           nc, pitch, shift=shift_n, scale=scale_n)
    return x
```
